```python
import math
import jax, jax.numpy as jnp
from jax import lax
import numpy as np

D_MODEL = 1024
BATCH = 8
SEQ = 2048
DEPTH = 2
DEC_BATCH = 128
DEC_SEQ = 4
PAST_LEN = 16384
PAGE_SIZE = 128

W_A = D_MODEL
CONV_A = 31
HG_HEADS = 8
HG_DK = 128
HG_DV = D_MODEL // HG_HEADS
W_BK = HG_HEADS * HG_DK
W_BV = HG_HEADS * HG_DV
HG_CHUNK = 64
W_C = D_MODEL
CONV_C = 3
N_BRANCH = 3
EPS = 1e-6

kernel_name = "hybrid_conformer_hgrn2_shortconv_step"


def _col_sizes():
    return [W_A, W_A, W_A, W_BK, W_BK, W_BV, W_BV, W_C, W_C, W_C, W_C, N_BRANCH * D_MODEL]


def _split_cols(p):
    idx = np.cumsum(_col_sizes())[:-1].tolist()
    return jnp.split(p, idx, axis=-1)


def rmsnorm(x, g):
    xf = x.astype(jnp.float32)
    y = xf * lax.rsqrt(jnp.mean(xf * xf, axis=-1, keepdims=True) + EPS) * g.astype(jnp.float32)
    return y.astype(x.dtype)


def layernorm(x, g, b):
    xf = x.astype(jnp.float32)
    mu = jnp.mean(xf, axis=-1, keepdims=True)
    xc = xf - mu
    var = jnp.mean(xc * xc, axis=-1, keepdims=True)
    y = xc * lax.rsqrt(var + EPS) * g.astype(jnp.float32) + b.astype(jnp.float32)
    return y.astype(x.dtype)


def causal_dwconv(x, prev, w, b=None):
    k, c = w.shape
    xp = jnp.concatenate([prev.astype(x.dtype), x], axis=1)
    y = lax.conv_general_dilated(xp, w.astype(x.dtype)[:, None, :], window_strides=(1,), padding='VALID',
                                 dimension_numbers=('NWC', 'WIO', 'NWC'), feature_group_count=c)
    if b is not None:
        y = y + b.astype(x.dtype)
    return y, xp[:, -(k - 1):]


def hgrn2_recurrence(q, logf, k, v, s0):
    bsz, t = q.shape[0], q.shape[1]
    c = math.gcd(t, HG_CHUNK)
    n = t // c

    def to_chunks(a):
        return a.astype(jnp.float32).reshape(bsz, n, c, a.shape[2], a.shape[3]).transpose(1, 0, 3, 2, 4)

    qc, lc, kc, vc = to_chunks(q), to_chunks(logf), to_chunks(k), to_chunks(v)
    mask = jnp.tril(jnp.ones((c, c), dtype=bool))[:, :, None]

    def step(s, inp):
        qi, li, ki, vi = inp
        bcum = jnp.cumsum(li, axis=2)
        inter = jnp.einsum('bhck,bhkv->bhcv', qi * jnp.exp(bcum), s)
        diff = bcum[:, :, :, None, :] - bcum[:, :, None, :, :]
        decay = jnp.exp(jnp.where(mask, diff, -jnp.inf))
        att = jnp.einsum('bhtsk,bhsk->bhts', qi[:, :, :, None, :] * decay, ki)
        o = inter + jnp.einsum('bhts,bhsv->bhtv', att, vi)
        blast = bcum[:, :, -1]
        s_new = jnp.exp(blast)[..., None] * s + jnp.einsum(
            'bhsk,bhsv->bhkv', ki * jnp.exp(blast[:, :, None, :] - bcum), vi)
        return s_new, o

    s_fin, o = lax.scan(step, s0.astype(jnp.float32), (qc, lc, kc, vc))
    o = o.transpose(1, 0, 3, 2, 4).reshape(bsz, t, q.shape[2], v.shape[3])
    return o, s_fin


def mixer_layer(x, prev_a, s0, prev_c, norm_g, w_in, gate_b, conv_a_w, conv_a_b, ln_g, ln_b, w_a_out,
                lb, hg_norm_g, w_b_out, conv_c_w, w_c_out, w_o):
    bsz, t, _ = x.shape
    h = rmsnorm(x, norm_g)
    proj = jnp.einsum('btd,dn->btn', h, w_in.astype(h.dtype))
    a_val, a_gate, z_a, q, f_pre, i_v, z_b, b_g, c_g, h_c, z_c, g_pre = _split_cols(proj)

    u = a_val * jax.nn.sigmoid(a_gate)
    ca, new_a = causal_dwconv(u, prev_a, conv_a_w, conv_a_b)
    ya = jax.nn.silu(layernorm(ca, ln_g, ln_b)) * jax.nn.silu(z_a)
    ya = ya @ w_a_out.astype(ya.dtype)

    q = q.reshape(bsz, t, HG_HEADS, HG_DK)
    f_pre = f_pre.reshape(bsz, t, HG_HEADS, HG_DK).astype(jnp.float32)
    i_v = i_v.reshape(bsz, t, HG_HEADS, HG_DV)
    lbh = lb.reshape(HG_HEADS, HG_DK)
    logf = jnp.logaddexp(jnp.log(lbh), jnp.log1p(-lbh) + jax.nn.log_sigmoid(f_pre))
    k = -jnp.expm1(logf)
    o, s_new = hgrn2_recurrence(q, logf, k, i_v, s0)
    o = o * lax.rsqrt(jnp.mean(o * o, axis=-1, keepdims=True) + EPS)
    o = (o.reshape(bsz, t, W_BV) * hg_norm_g.astype(jnp.float32)).astype(x.dtype)
    yb = (o * jax.nn.silu(z_b)) @ w_b_out.astype(x.dtype)

    cc, new_c = causal_dwconv(c_g * h_c, prev_c, conv_c_w)
    yc = (b_g * cc * jax.nn.silu(z_c)) @ w_c_out.astype(x.dtype)

    ga, gb, gc = jnp.split(jax.nn.sigmoid(g_pre + gate_b.astype(g_pre.dtype)), N_BRANCH, axis=-1)
    m = ga * ya + gb * yb + gc * yc
    x = x + m @ w_o.astype(m.dtype)
    return x, new_a, s_new.astype(x.dtype), new_c


def setup_inputs(seed: int = 0) -> dict:
    key = jax.random.key(seed)
    ks = jax.random.split(key, 20)
    n_cols = sum(_col_sizes())
    f32 = jnp.float32
    nrm = lambda k, shape, s: (jax.random.normal(k, shape, f32) * s)
    return {
        "x_prompt": nrm(ks[0], (BATCH, SEQ, D_MODEL), 1.0),
        "x_sample": nrm(ks[1], (DEC_BATCH, DEC_SEQ, D_MODEL), 1.0),
        "state_conv_a": nrm(ks[2], (DEPTH, DEC_BATCH, CONV_A - 1, W_A), 1.0),
        "state_hgrn": nrm(ks[3], (DEPTH, DEC_BATCH, HG_HEADS, HG_DK, HG_DV), 0.5),
        "state_conv_c": nrm(ks[4], (DEPTH, DEC_BATCH, CONV_C - 1, W_C), 1.0),
        "norm_g": 1.0 + nrm(ks[5], (DEPTH, D_MODEL), 0.01),
        "w_in": nrm(ks[6], (DEPTH, D_MODEL, n_cols), D_MODEL ** -0.5),
        "gate_b": nrm(ks[7], (DEPTH, N_BRANCH * D_MODEL), 0.01),
        "conv_a_w": nrm(ks[8], (DEPTH, CONV_A, W_A), CONV_A ** -0.5),
        "conv_a_b": nrm(ks[9], (DEPTH, W_A), 0.01),
        "ln_g": 1.0 + nrm(ks[10], (DEPTH, W_A), 0.01),
        "ln_b": nrm(ks[11], (DEPTH, W_A), 0.01),
        "w_a_out": nrm(ks[12], (DEPTH, W_A, D_MODEL), W_A ** -0.5),
        "lb_raw": nrm(ks[13], (DEPTH, W_BK), 0.5),
        "hg_norm_g": 1.0 + nrm(ks[14], (DEPTH, W_BV), 0.01),
        "w_b_out": nrm(ks[15], (DEPTH, W_BV, D_MODEL), W_BV ** -0.5),
        "conv_c_w": nrm(ks[16], (DEPTH, CONV_C, W_C), CONV_C ** -0.5),
        "w_c_out": nrm(ks[17], (DEPTH, W_C, D_MODEL), W_C ** -0.5),
        "w_o": nrm(ks[18], (DEPTH, D_MODEL, D_MODEL), D_MODEL ** -0.5),
        "final_norm_g": 1.0 + nrm(ks[19], (D_MODEL,), 0.01),
    }


def reference(x_prompt, x_sample, state_conv_a, state_hgrn, state_conv_c, norm_g, w_in, gate_b, conv_a_w,
              conv_a_b, ln_g, ln_b, w_a_out, lb_raw, hg_norm_g, w_b_out, conv_c_w, w_c_out, w_o, final_norm_g):
    lb_all = jnp.cumsum(jax.nn.softmax(lb_raw.astype(jnp.float32), axis=0), axis=0)
    lb_all = lb_all - lb_all[0:1]

    xp, xs = x_prompt, x_sample
    pa, ph, pc, sa, sh, sc = [], [], [], [], [], []
    for l in range(DEPTH):
        w = (norm_g[l], w_in[l], gate_b[l], conv_a_w[l], conv_a_b[l], ln_g[l], ln_b[l], w_a_out[l],
             lb_all[l], hg_norm_g[l], w_b_out[l], conv_c_w[l], w_c_out[l], w_o[l])
        zero_a = jnp.zeros((BATCH, CONV_A - 1, W_A), xp.dtype)
        zero_h = jnp.zeros((BATCH, HG_HEADS, HG_DK, HG_DV), jnp.float32)
        zero_c = jnp.zeros((BATCH, CONV_C - 1, W_C), xp.dtype)
        xp, na, nh, nc = mixer_layer(xp, zero_a, zero_h, zero_c, *w)
        pa.append(na); ph.append(nh); pc.append(nc)
        xs, na, nh, nc = mixer_layer(xs, state_conv_a[l], state_hgrn[l], state_conv_c[l], *w)
        sa.append(na); sh.append(nh); sc.append(nc)

    y_prompt = rmsnorm(xp, final_norm_g)
    y_sample = rmsnorm(xs, final_norm_g)
    return (y_prompt, y_sample, jnp.stack(pa), jnp.stack(ph), jnp.stack(pc),
            jnp.stack(sa), jnp.stack(sh), jnp.stack(sc))
```

```python
import functools

import numpy as np
import jax
import jax.numpy as jnp
from jax import lax
from jax.experimental import pallas as pl
from jax.experimental.pallas import tpu as pltpu

F32 = jnp.float32
BF16 = jnp.bfloat16

EPS = 1e-6
D_MODEL = 1024
HG_HEADS = 8
HG_DK = 128
HG_DV = 128
CONV_A = 31
CONV_C = 3
N_BRANCH = 3

SUBLANES = 8
LANES = 128
HALO_A = 32
HALO_C = 8
CONV_ROWS = 64
HG_CHUNK = 64
VMEM_LIMIT = 56 * 1024 * 1024

NT_DIMS = (((1,), (1,)), ((), ()))
TN_DIMS = (((0,), (0,)), ((), ()))


def _sigmoid(x):
    return 1.0 / (1.0 + jnp.exp(-x))


def _silu(x):
    return x * _sigmoid(x)


def _rms_rows(x, g):
    ms = jnp.mean(x * x, axis=-1, keepdims=True)
    return x * lax.rsqrt(ms + EPS) * g


def _layernorm_rows(x, g, b):
    mu = jnp.mean(x, axis=-1, keepdims=True)
    xc = x - mu
    var = jnp.mean(xc * xc, axis=-1, keepdims=True)
    return xc * lax.rsqrt(var + EPS) * g + b


def _dot(a, b):
    return jnp.dot(a, b, preferred_element_type=F32)


def _params(sem):
    return pltpu.CompilerParams(dimension_semantics=sem, vmem_limit_bytes=VMEM_LIMIT)


def _full(shape):
    n = len(shape)
    return pl.BlockSpec(shape, lambda *_: (0,) * n)


def _mixer_a_kernel(x_ref, g_ref, w_ref, cw_ref, cb_ref, lng_ref, lnb_ref, wo_ref,
                    ya_ref, na_ref, ubuf, cabuf, *, tt, nt):
    t = pl.program_id(1)

    @pl.when(t == 0)
    def _():
        ubuf[0:HALO_A, :] = jnp.zeros((HALO_A, D_MODEL), F32)

    h = _rms_rows(x_ref[0], g_ref[...]).astype(BF16)
    p = _dot(h, w_ref[...])
    ubuf[HALO_A:HALO_A + tt, :] = p[:, :D_MODEL] * _sigmoid(p[:, D_MODEL:2 * D_MODEL])
    gz = _silu(p[:, 2 * D_MODEL:])

    off = HALO_A - (CONV_A - 1)

    def lane_tile(c, carry):
        cs = pl.ds(pl.multiple_of(c * LANES, LANES), LANES)
        for rb in range(tt // CONV_ROWS):
            r0 = rb * CONV_ROWS
            acc = jnp.broadcast_to(cb_ref[:, cs], (CONV_ROWS, LANES))
            for j in range(CONV_A):
                acc = acc + cw_ref[j:j + 1, cs] * ubuf[r0 + off + j:r0 + off + j + CONV_ROWS, cs]
            cabuf[r0:r0 + CONV_ROWS, cs] = acc
        return carry

    lax.fori_loop(0, D_MODEL // LANES, lane_tile, 0)

    @pl.when(t == nt - 1)
    def _():
        na_ref[0] = ubuf[tt + off:tt + HALO_A, :]

    ubuf[0:HALO_A, :] = ubuf[tt:tt + HALO_A, :]

    y = _silu(_layernorm_rows(cabuf[...], lng_ref[...], lnb_ref[...])) * gz
    ya_ref[0] = _dot(y.astype(BF16), wo_ref[...])


def _mixer_a_prompt(x, norm_g, w, cw, cb, lng, lnb, wo, tt):
    bsz, seq, d = x.shape
    nt = seq // tt
    kern = functools.partial(_mixer_a_kernel, tt=tt, nt=nt)
    return pl.pallas_call(
        kern,
        grid=(bsz, nt),
        in_specs=[
            pl.BlockSpec((1, tt, d), lambda b, t: (b, t, 0)),
            _full((1, d)), _full(w.shape), _full(cw.shape), _full((1, d)), _full((1, d)), _full((1, d)),
            _full(wo.shape),
        ],
        out_specs=[
            pl.BlockSpec((1, tt, d), lambda b, t: (b, t, 0)),
            pl.BlockSpec((1, CONV_A - 1, d), lambda b, t: (b, 0, 0)),
        ],
        out_shape=[
            jax.ShapeDtypeStruct((bsz, seq, d), F32),
            jax.ShapeDtypeStruct((bsz, CONV_A - 1, d), F32),
        ],
        scratch_shapes=[
            pltpu.VMEM((HALO_A + tt, d), F32),
            pltpu.VMEM((tt, d), F32),
        ],
        compiler_params=_params(("arbitrary", "arbitrary")),
        name="mixer_a_prompt",
    )(x, norm_g, w, cw, cb, lng, lnb, wo)


def _mixer_a_sample_kernel(x_ref, st_ref, g_ref, w_ref, cw_ref, cb_ref, lng_ref, lnb_ref, wo_ref,
                           ya_ref, na_ref, *, steps, nb):
    x = x_ref[...].reshape(steps * nb, D_MODEL)
    h = _rms_rows(x, g_ref[...]).astype(BF16)
    p = _dot(h, w_ref[...])
    u = p[:, :D_MODEL] * _sigmoid(p[:, D_MODEL:2 * D_MODEL])
    gz = _silu(p[:, 2 * D_MODEL:])
    past = CONV_A - 1

    acc = [jnp.broadcast_to(cb_ref[...], (nb, D_MODEL)) for _ in range(steps)]
    for i in range(past + steps):
        slab = st_ref[:, i, :] if i < past else u[(i - past) * nb:(i - past + 1) * nb, :]
        for s in range(steps):
            j = i - s
            if 0 <= j < CONV_A:
                acc[s] = acc[s] + cw_ref[j:j + 1, :] * slab
    ca = jnp.concatenate(acc, axis=0)

    na_ref[:, 0:past - steps, :] = st_ref[:, steps:past, :]
    for s in range(steps):
        na_ref[:, past - steps + s, :] = u[s * nb:(s + 1) * nb, :]

    y = _silu(_layernorm_rows(ca, lng_ref[...], lnb_ref[...])) * gz
    ya_ref[...] = _dot(y.astype(BF16), wo_ref[...]).reshape(steps, nb, D_MODEL)


def _mixer_a_sample(x_tm, state, norm_g, w, cw, cb, lng, lnb, wo, nb):
    steps, n, d = x_tm.shape
    kern = functools.partial(_mixer_a_sample_kernel, steps=steps, nb=nb)
    return pl.pallas_call(
        kern,
        grid=(n // nb,),
        in_specs=[
            pl.BlockSpec((steps, nb, d), lambda i: (0, i, 0)),
            pl.BlockSpec((nb, CONV_A - 1, d), lambda i: (i, 0, 0)),
            _full((1, d)), _full(w.shape), _full(cw.shape), _full((1, d)), _full((1, d)), _full((1, d)),
            _full(wo.shape),
        ],
        out_specs=[
            pl.BlockSpec((steps, nb, d), lambda i: (0, i, 0)),
            pl.BlockSpec((nb, CONV_A - 1, d), lambda i: (i, 0, 0)),
        ],
        out_shape=[
            jax.ShapeDtypeStruct((steps, n, d), F32),
            jax.ShapeDtypeStruct((n, CONV_A - 1, d), F32),
        ],
        compiler_params=_params(("arbitrary",)),
        name="mixer_a_sample",
    )(x_tm, state, norm_g, w, cw, cb, lng, lnb, wo)


def _mixer_c_kernel(x_ref, g_ref, w_ref, cw_ref, wo_ref, yc_ref, nc_ref, vbuf, *, tt, nt):
    t = pl.program_id(1)

    @pl.when(t == 0)
    def _():
        vbuf[0:HALO_C, :] = jnp.zeros((HALO_C, D_MODEL), F32)

    h = _rms_rows(x_ref[0], g_ref[...]).astype(BF16)
    p = _dot(h, w_ref[...])
    vbuf[HALO_C:HALO_C + tt, :] = p[:, D_MODEL:2 * D_MODEL] * p[:, 2 * D_MODEL:3 * D_MODEL]
    off = HALO_C - (CONV_C - 1)
    cc = cw_ref[0:1, :] * vbuf[off:off + tt, :]
    for j in range(1, CONV_C):
        cc = cc + cw_ref[j:j + 1, :] * vbuf[off + j:off + j + tt, :]

    @pl.when(t == nt - 1)
    def _():
        nc_ref[0] = vbuf[tt + off:tt + HALO_C, :]

    vbuf[0:HALO_C, :] = vbuf[tt:tt + HALO_C, :]
    y = p[:, :D_MODEL] * cc * _silu(p[:, 3 * D_MODEL:])
    yc_ref[0] = _dot(y.astype(BF16), wo_ref[...])


def _mixer_c_prompt(x, norm_g, w, cw, wo, tt):
    bsz, seq, d = x.shape
    nt = seq // tt
    kern = functools.partial(_mixer_c_kernel, tt=tt, nt=nt)
    return pl.pallas_call(
        kern,
        grid=(bsz, nt),
        in_specs=[
            pl.BlockSpec((1, tt, d), lambda b, t: (b, t, 0)),
            _full((1, d)), _full(w.shape), _full(cw.shape), _full(wo.shape),
        ],
        out_specs=[
            pl.BlockSpec((1, tt, d), lambda b, t: (b, t, 0)),
            pl.BlockSpec((1, CONV_C - 1, d), lambda b, t: (b, 0, 0)),
        ],
        out_shape=[
            jax.ShapeDtypeStruct((bsz, seq, d), F32),
            jax.ShapeDtypeStruct((bsz, CONV_C - 1, d), F32),
        ],
        scratch_shapes=[pltpu.VMEM((HALO_C + tt, d), F32)],
        compiler_params=_params(("arbitrary", "arbitrary")),
        name="mixer_c_prompt",
    )(x, norm_g, w, cw, wo)


def _mixer_c_sample_kernel(x_ref, st_ref, g_ref, w_ref, cw_ref, wo_ref, yc_ref, nc_ref, *, steps, n):
    h = _rms_rows(x_ref[...], g_ref[...]).astype(BF16)
    p = _dot(h, w_ref[...])
    v = p[:, D_MODEL:2 * D_MODEL] * p[:, 2 * D_MODEL:3 * D_MODEL]
    past = CONV_C - 1
    slabs = [st_ref[:, i, :] for i in range(past)] + [v[s * n:(s + 1) * n, :] for s in range(steps)]
    rows = []
    for s in range(steps):
        cc = cw_ref[0:1, :] * slabs[s]
        for j in range(1, CONV_C):
            cc = cc + cw_ref[j:j + 1, :] * slabs[s + j]
        rows.append(cc)
    cc = jnp.concatenate(rows, axis=0)
    for i in range(past):
        nc_ref[:, i, :] = slabs[steps + i]
    y = p[:, :D_MODEL] * cc * _silu(p[:, 3 * D_MODEL:])
    yc_ref[...] = _dot(y.astype(BF16), wo_ref[...])


def _mixer_c_sample(x2, state, norm_g, w, cw, wo, steps):
    rows, d = x2.shape
    n = rows // steps
    kern = functools.partial(_mixer_c_sample_kernel, steps=steps, n=n)
    return pl.pallas_call(
        kern,
        grid=(1,),
        in_specs=[_full((rows, d)), _full(state.shape), _full((1, d)), _full(w.shape), _full(cw.shape),
                  _full(wo.shape)],
        out_specs=[_full((rows, d)), _full(state.shape)],
        out_shape=[jax.ShapeDtypeStruct((rows, d), F32), jax.ShapeDtypeStruct(state.shape, F32)],
        compiler_params=_params(("arbitrary",)),
        name="mixer_c_sample",
    )(x2, state, norm_g, w, cw, wo)


def _merge_kernel(x_ref, ya_ref, yb_ref, yc_ref, g_ref, w_ref, gb_ref, wo_ref, fg_ref, o_ref, *, final):
    x = x_ref[0]
    h = _rms_rows(x, g_ref[...]).astype(BF16)
    gate = _sigmoid(_dot(h, w_ref[...]) + gb_ref[...])
    m = (gate[:, :D_MODEL] * ya_ref[0] + gate[:, D_MODEL:2 * D_MODEL] * yb_ref[0]
         + gate[:, 2 * D_MODEL:] * yc_ref[0])
    out = x + _dot(m.astype(BF16), wo_ref[...])
    if final:
        out = _rms_rows(out, fg_ref[...])
    o_ref[0] = out


def _merge(x, ya, yb, yc, norm_g, w, gate_b, wo, final_g, tt, final):
    bsz, seq, d = x.shape
    tile = pl.BlockSpec((1, tt, d), lambda b, t: (b, t, 0))
    kern = functools.partial(_merge_kernel, final=final)
    return pl.pallas_call(
        kern,
        grid=(bsz, seq // tt),
        in_specs=[tile, tile, tile, tile, _full((1, d)), _full(w.shape), _full(gate_b.shape),
                  _full(wo.shape), _full((1, d))],
        out_specs=tile,
        out_shape=jax.ShapeDtypeStruct((bsz, seq, d), F32),
        compiler_params=_params(("arbitrary", "arbitrary")),
        name="merge",
    )(x, ya, yb, yc, norm_g, w, gate_b, wo, final_g)


def _lower_bound(lbraw_ref, layer):
    depth = lbraw_ref.shape[0]
    rows = [lbraw_ref[i:i + 1, :] for i in range(depth)]
    mx = functools.reduce(jnp.maximum, rows)
    ex = [jnp.exp(r - mx) for r in rows]
    tot = functools.reduce(lambda a, b: a + b, ex)
    csum = []
    run = None
    for e in ex:
        sm = e / tot
        run = sm if run is None else run + sm
        csum.append(run)
    return csum[layer] - csum[0]


def _forget_terms(f_pre, lb):
    e = jnp.exp(-jnp.abs(f_pre))
    log_sig = jnp.minimum(f_pre, 0.0) - jnp.log1p(e)
    log_lb = jnp.log(lb)
    b = jnp.log1p(-lb) + log_sig
    mx = jnp.maximum(log_lb, b)
    logf = mx + jnp.log1p(jnp.exp(-jnp.abs(log_lb - b)))
    k = (1.0 - lb) * (jnp.where(f_pre > 0.0, e, 1.0) / (1.0 + e))
    return logf, k


def _split3(x):
    hi = x.astype(BF16)
    r1 = x - hi.astype(F32)
    mid = r1.astype(BF16)
    lo = (r1 - mid.astype(F32)).astype(BF16)
    return hi, mid, lo


def _hgrn_chunk_consts(c):
    levels = c.bit_length() - 1
    t = np.arange(c)
    col = t[None, :]
    row = t[:, None]
    blocks = [col <= row, col > row]
    masks = []
    for j in range(levels):
        half = 1 << j
        p = ((t >> (j + 1)) << (j + 1)) + half
        tgt = ((t >> j) & 1) == 1
        seg_t = (col >= p[:, None]) & (col <= row)
        seg_s = (col > row) & (col <= p[:, None] - 1)
        blocks.append(np.where(tgt[:, None], seg_t, seg_s))
        same = (row >> (j + 1)) == (col >> (j + 1))
        masks.append(same & tgt[:, None] & (~tgt)[None, :])
    masks.append(np.eye(c, dtype=bool))
    assert (np.sum(np.stack(masks), axis=0) == (col <= row)).all()
    seg = np.concatenate(blocks, axis=0).astype(np.float32)
    seg3 = np.concatenate([seg, seg, seg], axis=1)
    return jnp.asarray(seg3, BF16), jnp.asarray(np.stack(masks).astype(np.float32)), levels


def _mixer_b_kernel(x_ref, g_ref, w_ref, lbraw_ref, hgn_ref, wo_ref, seg_ref, mask_ref,
                    yb_ref, nh_ref,
                    q_s, k_s, lf_s, v_s, gz_s, y_s, dec_s, st_s, *, tt, nt, layer, levels):
    t = pl.program_id(1)
    c = HG_CHUNK
    w_bk = HG_HEADS * HG_DK

    @pl.when(t == 0)
    def _():
        st_s[...] = jnp.zeros(st_s.shape, F32)

    h = _rms_rows(x_ref[0], g_ref[...]).astype(BF16)
    p = _dot(h, w_ref[...])
    logf, kk = _forget_terms(p[:, w_bk:2 * w_bk], _lower_bound(lbraw_ref, layer))
    q_s[...] = p[:, :w_bk]
    lf_s[...] = logf
    k_s[...] = kk
    v_s[...] = p[:, 2 * w_bk:3 * w_bk].astype(BF16)
    gz_s[...] = _silu(p[:, 3 * w_bk:])

    def chunk(ci, carry):
        rows = pl.ds(pl.multiple_of(ci * c, c), c)
        hi, mid, lo = _split3(lf_s[rows, :])
        expo = _dot(seg_ref[...], jnp.concatenate([hi, mid, lo], axis=0))
        dec_s[...] = jnp.exp(expo)
        for hd in range(HG_HEADS):
            cs = slice(hd * HG_DK, (hd + 1) * HG_DK)
            qh = q_s[rows, cs]
            kh = k_s[rows, cs]
            vh = v_s[rows, cs]
            st = st_s[hd]
            qg = (qh * dec_s[0:c, cs]).astype(BF16)
            o = lax.dot_general(qg, st.astype(BF16), NT_DIMS, preferred_element_type=F32)
            att = lax.dot_general(qh.astype(BF16), kh.astype(BF16), NT_DIMS,
                                  preferred_element_type=F32) * mask_ref[levels]
            for j in range(levels):
                gj = dec_s[(2 + j) * c:(3 + j) * c, cs]
                pj = lax.dot_general((qh * gj).astype(BF16), (kh * gj).astype(BF16), NT_DIMS,
                                     preferred_element_type=F32)
                att = att + pj * mask_ref[j]
            o = o + _dot(att.astype(BF16), vh)
            khat = (kh * dec_s[c:2 * c, cs]).astype(BF16)
            st_s[hd] = st * dec_s[c - 1:c, cs] + lax.dot_general(vh, khat, TN_DIMS,
                                                                 preferred_element_type=F32)
            o = o * lax.rsqrt(jnp.mean(o * o, axis=-1, keepdims=True) + EPS) * hgn_ref[:, cs]
            y_s[rows, cs] = (o * gz_s[rows, cs]).astype(BF16)
        return carry

    lax.fori_loop(0, tt // c, chunk, 0)
    yb_ref[0] = _dot(y_s[...], wo_ref[...])

    @pl.when(t == nt - 1)
    def _():
        for hd in range(HG_HEADS):
            nh_ref[0, hd] = st_s[hd].T


def _mixer_b_prompt(x, norm_g, w, lb_raw, hgn, wo, layer, tt):
    bsz, seq, d = x.shape
    nt = seq // tt
    seg3, masks, levels = _hgrn_chunk_consts(HG_CHUNK)
    w_bk = HG_HEADS * HG_DK
    kern = functools.partial(_mixer_b_kernel, tt=tt, nt=nt, layer=layer, levels=levels)
    return pl.pallas_call(
        kern,
        grid=(bsz, nt),
        in_specs=[
            pl.BlockSpec((1, tt, d), lambda b, t: (b, t, 0)),
            _full((1, d)), _full(w.shape), _full(lb_raw.shape), _full((1, w_bk)), _full(wo.shape),
            _full(seg3.shape), _full(masks.shape),
        ],
        out_specs=[
            pl.BlockSpec((1, tt, d), lambda b, t: (b, t, 0)),
            pl.BlockSpec((1, HG_HEADS, HG_DK, HG_DV), lambda b, t: (b, 0, 0, 0)),
        ],
        out_shape=[
            jax.ShapeDtypeStruct((bsz, seq, d), F32),
            jax.ShapeDtypeStruct((bsz, HG_HEADS, HG_DK, HG_DV), F32),
        ],
        scratch_shapes=[
            pltpu.VMEM((tt, w_bk), F32),
            pltpu.VMEM((tt, w_bk), F32),
            pltpu.VMEM((tt, w_bk), F32),
            pltpu.VMEM((tt, w_bk), BF16),
            pltpu.VMEM((tt, w_bk), F32),
            pltpu.VMEM((tt, w_bk), BF16),
            pltpu.VMEM(((levels + 2) * HG_CHUNK, w_bk), F32),
            pltpu.VMEM((HG_HEADS, HG_DV, HG_DK), F32),
        ],
        compiler_params=_params(("arbitrary", "arbitrary")),
        name="mixer_b_prompt",
    )(x, norm_g, w, lb_raw, hgn, wo, seg3, masks)


def _mixer_b_sample_proj_kernel(x_ref, g_ref, w_ref, lbraw_ref,
                                qg_ref, kh_ref, v_ref, eb_ref, oi_ref, gz_ref, *, steps, n, layer):
    w_bk = HG_HEADS * HG_DK
    h = _rms_rows(x_ref[...], g_ref[...]).astype(BF16)
    p = _dot(h, w_ref[...])
    logf, kk = _forget_terms(p[:, w_bk:2 * w_bk], _lower_bound(lbraw_ref, layer))
    q = p[:, :w_bk]
    v = p[:, 2 * w_bk:3 * w_bk]
    gz_ref[...] = _silu(p[:, 3 * w_bk:])
    v_ref[...] = v.astype(BF16)

    rs = lambda a, s: a[s * n:(s + 1) * n, :]
    bc = []
    for s in range(steps):
        bc.append(rs(logf, s) if s == 0 else bc[-1] + rs(logf, s))
    eb_ref[...] = jnp.exp(bc[-1])
    ones = jnp.ones((HG_DK, HG_DK), BF16)
    for s in range(steps):
        qg_ref[s * n:(s + 1) * n, :] = (rs(q, s) * jnp.exp(bc[s])).astype(BF16)
        kh_ref[s * n:(s + 1) * n, :] = (rs(kk, s) * jnp.exp(bc[-1] - bc[s])).astype(BF16)
        acc = jnp.zeros((n, w_bk), F32)
        for r in range(s + 1):
            pr = rs(q, s) * rs(kk, r)
            if r < s:
                pr = pr * jnp.exp(bc[s] - bc[r])
            pr = pr.astype(BF16)
            att = jnp.concatenate(
                [_dot(pr[:, hd * HG_DK:(hd + 1) * HG_DK], ones) for hd in range(HG_HEADS)], axis=1)
            acc = acc + att * rs(v, r)
        oi_ref[s * n:(s + 1) * n, :] = acc


def _mixer_b_sample_proj(x2, norm_g, w, lb_raw, layer, steps):
    rows, d = x2.shape
    n = rows // steps
    w_bk = HG_HEADS * HG_DK
    kern = functools.partial(_mixer_b_sample_proj_kernel, steps=steps, n=n, layer=layer)
    return pl.pallas_call(
        kern,
        grid=(1,),
        in_specs=[_full((rows, d)), _full((1, d)), _full(w.shape), _full(lb_raw.shape)],
        out_specs=[_full((rows, w_bk)), _full((rows, w_bk)), _full((rows, w_bk)), _full((n, w_bk)),
                   _full((rows, w_bk)), _full((rows, w_bk))],
        out_shape=[
            jax.ShapeDtypeStruct((rows, w_bk), BF16),
            jax.ShapeDtypeStruct((rows, w_bk), BF16),
            jax.ShapeDtypeStruct((rows, w_bk), BF16),
            jax.ShapeDtypeStruct((n, w_bk), F32),
            jax.ShapeDtypeStruct((rows, w_bk), F32),
            jax.ShapeDtypeStruct((rows, w_bk), F32),
        ],
        compiler_params=_params(("arbitrary",)),
        name="mixer_b_sample_proj",
    )(x2, norm_g, w, lb_raw)


def _mixer_b_sample_state_kernel(st_ref, qg_ref, kh_ref, v_ref, eb_ref, io_ref, ns_ref, *, steps, nb):
    pad = SUBLANES - 3 - steps
    ones3 = jnp.ones((3, HG_DV), BF16)
    for i in range(nb):
        for hd in range(HG_HEADS):
            cs = slice(hd * HG_DK, (hd + 1) * HG_DK)
            s = st_ref[i, hd]
            io_ref[:, i, cs] = _dot(qg_ref[:, i, cs], s.astype(BF16))
            hi, mid, lo = _split3(eb_ref[i:i + 1, cs])
            lhs = jnp.concatenate([hi, mid, lo, kh_ref[:, i, cs], jnp.zeros((pad, HG_DK), BF16)], axis=0)
            top = jnp.concatenate([ones3, jnp.zeros((3, HG_DV), BF16)], axis=1)
            mid_rows = jnp.concatenate([jnp.zeros((steps, HG_DV), BF16), v_ref[:, i, cs]], axis=1)
            rhs = jnp.concatenate([top, mid_rows, jnp.zeros((pad, 2 * HG_DV), BF16)], axis=0)
            r = lax.dot_general(lhs, rhs, TN_DIMS, preferred_element_type=F32)
            ns_ref[i, hd] = r[:, :HG_DV] * s + r[:, HG_DV:]


def _mixer_b_sample_state(state, qg, kh, v, eb, nb):
    steps, n, w_bk = qg.shape
    kern = functools.partial(_mixer_b_sample_state_kernel, steps=steps, nb=nb)
    tok = pl.BlockSpec((steps, nb, w_bk), lambda i: (0, i, 0))
    st = pl.BlockSpec((nb, HG_HEADS, HG_DK, HG_DV), lambda i: (i, 0, 0, 0))
    return pl.pallas_call(
        kern,
        grid=(n // nb,),
        in_specs=[st, tok, tok, tok, pl.BlockSpec((nb, w_bk), lambda i: (i, 0))],
        out_specs=[tok, st],
        out_shape=[jax.ShapeDtypeStruct((steps, n, w_bk), F32), jax.ShapeDtypeStruct(state.shape, F32)],
        compiler_params=_params(("arbitrary",)),
        name="mixer_b_sample_state",
    )(state, qg, kh, v, eb)


def _mixer_b_sample_out_kernel(io_ref, oi_ref, gz_ref, hgn_ref, wo_ref, yb_ref):
    o = io_ref[...] + oi_ref[...]
    parts = []
    for hd in range(HG_HEADS):
        oh = o[:, hd * HG_DV:(hd + 1) * HG_DV]
        parts.append(oh * lax.rsqrt(jnp.mean(oh * oh, axis=-1, keepdims=True) + EPS))
    y = jnp.concatenate(parts, axis=1) * hgn_ref[...] * gz_ref[...]
    yb_ref[...] = _dot(y.astype(BF16), wo_ref[...])


def _mixer_b_sample_out(inter, intra, gz, hgn, wo):
    rows, w_bv = inter.shape
    return pl.pallas_call(
        _mixer_b_sample_out_kernel,
        grid=(1,),
        in_specs=[_full((rows, w_bv))] * 3 + [_full((1, w_bv)), _full(wo.shape)],
        out_specs=_full((rows, wo.shape[1])),
        out_shape=jax.ShapeDtypeStruct((rows, wo.shape[1]), F32),
        compiler_params=_params(("arbitrary",)),
        name="mixer_b_sample_out",
    )(inter, intra, gz, hgn, wo)


def kernel(x_prompt, x_sample, state_conv_a, state_hgrn, state_conv_c, norm_g, w_in, gate_b, conv_a_w,
           conv_a_b, ln_g, ln_b, w_a_out, lb_raw, hg_norm_g, w_b_out, conv_c_w, w_c_out, w_o, final_norm_g):
    depth = w_in.shape[0]
    d = x_prompt.shape[-1]
    n_seq, steps, _ = x_sample.shape
    w_bk = HG_HEADS * HG_DK
    w_bv = HG_HEADS * HG_DV
    col_a = 3 * d
    col_b = col_a + 2 * w_bk + 2 * w_bv
    col_c = col_b + 4 * d
    row = lambda a: a.reshape(1, -1)
    fin_g = row(final_norm_g)

    xp = x_prompt
    xs = jnp.swapaxes(x_sample, 0, 1)
    pa, ph, pc, sa, sh, sc = [], [], [], [], [], []
    for l in range(depth):
        wl = w_in[l].astype(BF16)
        w_a, w_b, w_c, w_g = wl[:, :col_a], wl[:, col_a:col_b], wl[:, col_b:col_c], wl[:, col_c:]
        wao, wbo, wco, woo = (w_a_out[l].astype(BF16), w_b_out[l].astype(BF16), w_c_out[l].astype(BF16),
                              w_o[l].astype(BF16))
        ng, gb = row(norm_g[l]), row(gate_b[l])
        cab, lng, lnb, hgn = row(conv_a_b[l]), row(ln_g[l]), row(ln_b[l]), row(hg_norm_g[l])
        final = l == depth - 1

        ya, na = _mixer_a_prompt(xp, ng, w_a, conv_a_w[l], cab, lng, lnb, wao, tt=512)
        yb, nh = _mixer_b_prompt(xp, ng, w_b, lb_raw, hgn, wbo, layer=l, tt=256)
        yc, nc = _mixer_c_prompt(xp, ng, w_c, conv_c_w[l], wco, tt=512)
        xp = _merge(xp, ya, yb, yc, ng, w_g, gb, woo, fin_g, tt=512, final=final)
        pa.append(na); ph.append(nh); pc.append(nc)

        xs2 = xs.reshape(steps * n_seq, d)
        ya, na = _mixer_a_sample(xs, state_conv_a[l], ng, w_a, conv_a_w[l], cab, lng, lnb, wao, nb=32)
        qg, kh, v, eb, intra, gz = _mixer_b_sample_proj(xs2, ng, w_b, lb_raw, layer=l, steps=steps)
        tm = lambda a: a.reshape(steps, n_seq, w_bk)
        inter, nh = _mixer_b_sample_state(state_hgrn[l], tm(qg), tm(kh), tm(v), eb, nb=8)
        yb = _mixer_b_sample_out(inter.reshape(steps * n_seq, w_bv), intra, gz, hgn, wbo)
        yc, nc = _mixer_c_sample(xs2, state_conv_c[l], ng, w_c, conv_c_w[l], wco, steps=steps)
        tile = lambda a: a.reshape(1, steps * n_seq, d)
        xs = _merge(tile(xs), tile(ya), tile(yb), tile(yc), ng, w_g, gb, woo, fin_g,
                    tt=steps * n_seq, final=final).reshape(steps, n_seq, d)
        sa.append(na); sh.append(nh); sc.append(nc)

    return (xp, jnp.swapaxes(xs, 0, 1), jnp.stack(pa), jnp.stack(ph), jnp.stack(pc),
            jnp.stack(sa), jnp.stack(sh), jnp.stack(sc))
```

```python
import functools

import numpy as np
import jax
import jax.numpy as jnp
from jax import lax
from jax.experimental import pallas as pl
from jax.experimental.pallas import tpu as pltpu

F32 = jnp.float32
BF16 = jnp.bfloat16

EPS = 1e-6
D_MODEL = 1024
HG_HEADS = 8
HG_DK = 128
HG_DV = 128
CONV_A = 31
CONV_C = 3
N_BRANCH = 3

SUBLANES = 8
LANES = 128
HALO_A = 32
HALO_C = 8
CONV_ROWS = 64
HG_CHUNK = 64
VMEM_LIMIT = 56 * 1024 * 1024

NT_DIMS = (((1,), (1,)), ((), ()))
TN_DIMS = (((0,), (0,)), ((), ()))


def _sigmoid(x):
    return 1.0 / (1.0 + jnp.exp(-x))


def _silu(x):
    return x * _sigmoid(x)


def _rms_rows(x, g):
    ms = jnp.mean(x * x, axis=-1, keepdims=True)
    return x * lax.rsqrt(ms + EPS) * g


def _layernorm_rows(x, g, b):
    mu = jnp.mean(x, axis=-1, keepdims=True)
    xc = x - mu
    var = jnp.mean(xc * xc, axis=-1, keepdims=True)
    return xc * lax.rsqrt(var + EPS) * g + b


def _dot(a, b):
    return jnp.dot(a, b, preferred_element_type=F32)


def _params(sem):
    return pltpu.CompilerParams(dimension_semantics=sem, vmem_limit_bytes=VMEM_LIMIT)


def _full(shape):
    n = len(shape)
    return pl.BlockSpec(shape, lambda *_: (0,) * n)


def _stacked_slot(prev, n_inputs, out_index):
    if prev is None:
        return [], [], {}
    return [pl.BlockSpec(memory_space=pl.ANY)], [prev], {n_inputs: out_index}


def _mixer_a_kernel(x_ref, g_ref, w_ref, cw_ref, cb_ref, lng_ref, lnb_ref, wo_ref,
                    ya_ref, na_ref, ubuf, cabuf, shbuf, *, tt, nt):
    t = pl.program_id(1)

    @pl.when(t == 0)
    def _():
        ubuf[0:HALO_A, :] = jnp.zeros((HALO_A, D_MODEL), F32)

    h = _rms_rows(x_ref[0], g_ref[...]).astype(BF16)
    p = _dot(h, w_ref[...])
    ubuf[HALO_A:HALO_A + tt, :] = p[:, :D_MODEL] * _sigmoid(p[:, D_MODEL:2 * D_MODEL])
    gz = _silu(p[:, 2 * D_MODEL:])

    off = HALO_A - (CONV_A - 1)

    nsh = HALO_A + tt - SUBLANES

    def lane_tile(c, carry):
        cs = pl.ds(pl.multiple_of(c * LANES, LANES), LANES)
        for r in range(1, SUBLANES):
            shbuf[r - 1, :, cs] = ubuf[r:r + nsh, cs]
        for rb in range(tt // CONV_ROWS):
            r0 = rb * CONV_ROWS
            acc = jnp.broadcast_to(cb_ref[:, cs], (CONV_ROWS, LANES))
            for j in range(CONV_A):
                base, r = divmod(off + j, SUBLANES)
                lo = r0 + base * SUBLANES
                win = ubuf[lo:lo + CONV_ROWS, cs] if r == 0 else shbuf[r - 1, lo:lo + CONV_ROWS, cs]
                acc = acc + cw_ref[j:j + 1, cs] * win
            cabuf[r0:r0 + CONV_ROWS, cs] = acc
        return carry

    lax.fori_loop(0, D_MODEL // LANES, lane_tile, 0)

    @pl.when(t == nt - 1)
    def _():
        na_ref[0] = ubuf[tt + off:tt + HALO_A, :]

    ubuf[0:HALO_A, :] = ubuf[tt:tt + HALO_A, :]

    y = _silu(_layernorm_rows(cabuf[...], lng_ref[...], lnb_ref[...])) * gz
    ya_ref[0] = _dot(y.astype(BF16), wo_ref[...])


def _mixer_a_prompt(x, norm_g, w, cw, cb, lng, lnb, wo, tt):
    bsz, seq, d = x.shape
    nt = seq // tt
    kern = functools.partial(_mixer_a_kernel, tt=tt, nt=nt)
    return pl.pallas_call(
        kern,
        grid=(bsz, nt),
        in_specs=[
            pl.BlockSpec((1, tt, d), lambda b, t: (b, t, 0)),
            _full((1, d)), _full(w.shape), _full(cw.shape), _full((1, d)), _full((1, d)), _full((1, d)),
            _full(wo.shape),
        ],
        out_specs=[
            pl.BlockSpec((1, tt, d), lambda b, t: (b, t, 0)),
            pl.BlockSpec((1, CONV_A - 1, d), lambda b, t: (b, 0, 0)),
        ],
        out_shape=[
            jax.ShapeDtypeStruct((bsz, seq, d), F32),
            jax.ShapeDtypeStruct((bsz, CONV_A - 1, d), F32),
        ],
        scratch_shapes=[
            pltpu.VMEM((HALO_A + tt, d), F32),
            pltpu.VMEM((tt, d), F32),
            pltpu.VMEM((SUBLANES - 1, HALO_A + tt - SUBLANES, d), F32),
        ],
        compiler_params=_params(("arbitrary", "arbitrary")),
        name="mixer_a_prompt",
    )(x, norm_g, w, cw, cb, lng, lnb, wo)


def _mixer_a_sample_kernel(x_ref, st_ref, g_ref, w_ref, cw_ref, cb_ref, lng_ref, lnb_ref, wo_ref, *rest,
                           steps, nb):
    ya_ref, na_ref = rest[-2:]
    x = x_ref[...].reshape(steps * nb, D_MODEL)
    h = _rms_rows(x, g_ref[...]).astype(BF16)
    p = _dot(h, w_ref[...])
    u = p[:, :D_MODEL] * _sigmoid(p[:, D_MODEL:2 * D_MODEL])
    gz = _silu(p[:, 2 * D_MODEL:])
    past = CONV_A - 1

    acc = [jnp.broadcast_to(cb_ref[...], (nb, D_MODEL)) for _ in range(steps)]
    for i in range(past + steps):
        slab = st_ref[:, i, :] if i < past else u[(i - past) * nb:(i - past + 1) * nb, :]
        for s in range(steps):
            j = i - s
            if 0 <= j < CONV_A:
                acc[s] = acc[s] + cw_ref[j:j + 1, :] * slab
    ca = jnp.concatenate(acc, axis=0)

    na_ref[:, 0:past - steps, :] = st_ref[:, steps:past, :]
    for s in range(steps):
        na_ref[:, past - steps + s, :] = u[s * nb:(s + 1) * nb, :]

    y = _silu(_layernorm_rows(ca, lng_ref[...], lnb_ref[...])) * gz
    ya_ref[...] = _dot(y.astype(BF16), wo_ref[...]).reshape(steps, nb, D_MODEL)


def _mixer_a_sample(x_tm, states, norm_g, w, cw, cb, lng, lnb, wo, prev, layer, nb):
    steps, n, d = x_tm.shape
    kern = functools.partial(_mixer_a_sample_kernel, steps=steps, nb=nb)
    st = pl.BlockSpec((None, nb, CONV_A - 1, d), lambda i: (layer, i, 0, 0))
    operands = [x_tm, states, norm_g, w, cw, cb, lng, lnb, wo]
    slot_specs, slot_args, aliases = _stacked_slot(prev, len(operands), 1)
    return pl.pallas_call(
        kern,
        grid=(n // nb,),
        in_specs=[
            pl.BlockSpec((steps, nb, d), lambda i: (0, i, 0)),
            st,
            _full((1, d)), _full(w.shape), _full(cw.shape), _full((1, d)), _full((1, d)), _full((1, d)),
            _full(wo.shape),
        ] + slot_specs,
        out_specs=[pl.BlockSpec((steps, nb, d), lambda i: (0, i, 0)), st],
        out_shape=[jax.ShapeDtypeStruct((steps, n, d), F32), jax.ShapeDtypeStruct(states.shape, F32)],
        input_output_aliases=aliases,
        compiler_params=_params(("arbitrary",)),
        name="mixer_a_sample",
    )(*operands, *slot_args)


def _mixer_c_kernel(x_ref, g_ref, w_ref, cw_ref, wo_ref, yc_ref, nc_ref, vbuf, *, tt, nt):
    t = pl.program_id(1)

    @pl.when(t == 0)
    def _():
        vbuf[0:HALO_C, :] = jnp.zeros((HALO_C, D_MODEL), F32)

    h = _rms_rows(x_ref[0], g_ref[...]).astype(BF16)
    p = _dot(h, w_ref[...])
    vbuf[HALO_C:HALO_C + tt, :] = p[:, D_MODEL:2 * D_MODEL] * p[:, 2 * D_MODEL:3 * D_MODEL]
    off = HALO_C - (CONV_C - 1)
    cc = cw_ref[0:1, :] * vbuf[off:off + tt, :]
    for j in range(1, CONV_C):
        cc = cc + cw_ref[j:j + 1, :] * vbuf[off + j:off + j + tt, :]

    @pl.when(t == nt - 1)
    def _():
        nc_ref[0] = vbuf[tt + off:tt + HALO_C, :]

    vbuf[0:HALO_C, :] = vbuf[tt:tt + HALO_C, :]
    y = p[:, :D_MODEL] * cc * _silu(p[:, 3 * D_MODEL:])
    yc_ref[0] = _dot(y.astype(BF16), wo_ref[...])


def _mixer_c_prompt(x, norm_g, w, cw, wo, tt):
    bsz, seq, d = x.shape
    nt = seq // tt
    kern = functools.partial(_mixer_c_kernel, tt=tt, nt=nt)
    return pl.pallas_call(
        kern,
        grid=(bsz, nt),
        in_specs=[
            pl.BlockSpec((1, tt, d), lambda b, t: (b, t, 0)),
            _full((1, d)), _full(w.shape), _full(cw.shape), _full(wo.shape),
        ],
        out_specs=[
            pl.BlockSpec((1, tt, d), lambda b, t: (b, t, 0)),
            pl.BlockSpec((1, CONV_C - 1, d), lambda b, t: (b, 0, 0)),
        ],
        out_shape=[
            jax.ShapeDtypeStruct((bsz, seq, d), F32),
            jax.ShapeDtypeStruct((bsz, CONV_C - 1, d), F32),
        ],
        scratch_shapes=[pltpu.VMEM((HALO_C + tt, d), F32)],
        compiler_params=_params(("arbitrary", "arbitrary")),
        name="mixer_c_prompt",
    )(x, norm_g, w, cw, wo)


def _mixer_c_sample_kernel(x_ref, st_ref, g_ref, w_ref, cw_ref, wo_ref, yc_ref, nc_ref, *, steps, n):
    h = _rms_rows(x_ref[...], g_ref[...]).astype(BF16)
    p = _dot(h, w_ref[...])
    v = p[:, D_MODEL:2 * D_MODEL] * p[:, 2 * D_MODEL:3 * D_MODEL]
    past = CONV_C - 1
    slabs = [st_ref[:, i, :] for i in range(past)] + [v[s * n:(s + 1) * n, :] for s in range(steps)]
    rows = []
    for s in range(steps):
        cc = cw_ref[0:1, :] * slabs[s]
        for j in range(1, CONV_C):
            cc = cc + cw_ref[j:j + 1, :] * slabs[s + j]
        rows.append(cc)
    cc = jnp.concatenate(rows, axis=0)
    for i in range(past):
        nc_ref[:, i, :] = slabs[steps + i]
    y = p[:, :D_MODEL] * cc * _silu(p[:, 3 * D_MODEL:])
    yc_ref[...] = _dot(y.astype(BF16), wo_ref[...])


def _mixer_c_sample(x2, state, norm_g, w, cw, wo, steps):
    rows, d = x2.shape
    n = rows // steps
    kern = functools.partial(_mixer_c_sample_kernel, steps=steps, n=n)
    return pl.pallas_call(
        kern,
        grid=(1,),
        in_specs=[_full((rows, d)), _full(state.shape), _full((1, d)), _full(w.shape), _full(cw.shape),
                  _full(wo.shape)],
        out_specs=[_full((rows, d)), _full(state.shape)],
        out_shape=[jax.ShapeDtypeStruct((rows, d), F32), jax.ShapeDtypeStruct(state.shape, F32)],
        compiler_params=_params(("arbitrary",)),
        name="mixer_c_sample",
    )(x2, state, norm_g, w, cw, wo)


def _merge_kernel(x_ref, ya_ref, yb_ref, yc_ref, g_ref, w_ref, gb_ref, wo_ref, fg_ref, o_ref, *, final):
    x = x_ref[0]
    h = _rms_rows(x, g_ref[...]).astype(BF16)
    gate = _sigmoid(_dot(h, w_ref[...]) + gb_ref[...])
    m = (gate[:, :D_MODEL] * ya_ref[0] + gate[:, D_MODEL:2 * D_MODEL] * yb_ref[0]
         + gate[:, 2 * D_MODEL:] * yc_ref[0])
    out = x + _dot(m.astype(BF16), wo_ref[...])
    if final:
        out = _rms_rows(out, fg_ref[...])
    o_ref[0] = out


def _merge(x, ya, yb, yc, norm_g, w, gate_b, wo, final_g, tt, final):
    bsz, seq, d = x.shape
    tile = pl.BlockSpec((1, tt, d), lambda b, t: (b, t, 0))
    kern = functools.partial(_merge_kernel, final=final)
    return pl.pallas_call(
        kern,
        grid=(bsz, seq // tt),
        in_specs=[tile, tile, tile, tile, _full((1, d)), _full(w.shape), _full(gate_b.shape),
                  _full(wo.shape), _full((1, d))],
        out_specs=tile,
        out_shape=jax.ShapeDtypeStruct((bsz, seq, d), F32),
        compiler_params=_params(("arbitrary", "arbitrary")),
        name="merge",
    )(x, ya, yb, yc, norm_g, w, gate_b, wo, final_g)


def _lower_bound(lbraw_ref, layer):
    depth = lbraw_ref.shape[0]
    rows = [lbraw_ref[i:i + 1, :] for i in range(depth)]
    mx = functools.reduce(jnp.maximum, rows)
    ex = [jnp.exp(r - mx) for r in rows]
    tot = functools.reduce(lambda a, b: a + b, ex)
    csum = []
    run = None
    for e in ex:
        sm = e / tot
        run = sm if run is None else run + sm
        csum.append(run)
    return csum[layer] - csum[0]


def _forget_terms(f_pre, lb):
    e = jnp.exp(-jnp.abs(f_pre))
    log_sig = jnp.minimum(f_pre, 0.0) - jnp.log1p(e)
    log_lb = jnp.log(lb)
    b = jnp.log1p(-lb) + log_sig
    mx = jnp.maximum(log_lb, b)
    logf = mx + jnp.log1p(jnp.exp(-jnp.abs(log_lb - b)))
    k = (1.0 - lb) * (jnp.where(f_pre > 0.0, e, 1.0) / (1.0 + e))
    return logf, k


def _split3(x):
    hi = x.astype(BF16)
    r1 = x - hi.astype(F32)
    mid = r1.astype(BF16)
    lo = (r1 - mid.astype(F32)).astype(BF16)
    return hi, mid, lo


def _hgrn_chunk_consts(c):
    levels = c.bit_length() - 1
    t = np.arange(c)
    col = t[None, :]
    row = t[:, None]
    blocks = [col <= row, col > row]
    masks = []
    for j in range(levels):
        half = 1 << j
        p = ((t >> (j + 1)) << (j + 1)) + half
        tgt = ((t >> j) & 1) == 1
        seg_t = (col >= p[:, None]) & (col <= row)
        seg_s = (col > row) & (col <= p[:, None] - 1)
        blocks.append(np.where(tgt[:, None], seg_t, seg_s))
        same = (row >> (j + 1)) == (col >> (j + 1))
        masks.append(same & tgt[:, None] & (~tgt)[None, :])
    assert (np.sum(np.stack(masks), axis=0) == (col < row)).all()
    seg = np.concatenate(blocks, axis=0).astype(np.float32)
    seg3 = np.concatenate([seg, seg, seg], axis=1)
    pair_masks = np.tile(np.stack(masks).astype(np.float32), (1, 1, 2))
    return jnp.asarray(seg3, BF16), jnp.asarray(pair_masks), levels


def _mixer_b_kernel(x_ref, g_ref, w_ref, lbraw_ref, hgn_ref, wo_ref, seg_ref, mask_ref, *rest,
                    tt, nt, layer, levels):
    yb_ref, nh_ref, q_s, k_s, lf_s, v_s, gz_s, y_s, dec_s, att_s, st_s = rest[-11:]
    t = pl.program_id(1)
    c = HG_CHUNK
    w_bk = HG_HEADS * HG_DK

    @pl.when(t == 0)
    def _():
        st_s[...] = jnp.zeros(st_s.shape, F32)

    h = _rms_rows(x_ref[0], g_ref[...]).astype(BF16)
    p = _dot(h, w_ref[...])
    logf, kk = _forget_terms(p[:, w_bk:2 * w_bk], _lower_bound(lbraw_ref, layer))
    q_s[...] = p[:, :w_bk]
    lf_s[...] = logf
    k_s[...] = kk
    v_s[...] = p[:, 2 * w_bk:3 * w_bk].astype(BF16)
    gz_s[...] = _silu(p[:, 3 * w_bk:])

    def chunk(ci, carry):
        rows = pl.ds(pl.multiple_of(ci * c, c), c)
        hi, mid, lo = _split3(lf_s[rows, :])
        expo = _dot(seg_ref[...], jnp.concatenate([hi, mid, lo], axis=0))
        dec_s[...] = jnp.exp(expo)
        zrow = jnp.zeros((c, HG_DK), BF16)
        zblk = jnp.zeros((HG_DV, HG_DK), BF16)

        def blockdiag(m, z):
            return jnp.concatenate([jnp.concatenate([m[:, :HG_DK], z], axis=1),
                                    jnp.concatenate([z, m[:, HG_DK:]], axis=1)], axis=0)

        for pr in range(HG_HEADS // 2):
            cs2 = slice(2 * pr * HG_DK, 2 * (pr + 1) * HG_DK)
            q2 = q_s[rows, cs2]
            k2 = k_s[rows, cs2]
            att = None
            for j in range(levels):
                gj = dec_s[(2 + j) * c:(3 + j) * c, cs2]
                kj = (k2 * gj).astype(BF16)
                pj = lax.dot_general((q2 * gj).astype(BF16), blockdiag(kj, zrow), NT_DIMS,
                                     preferred_element_type=F32) * mask_ref[j]
                att = pj if att is None else att + pj
            att_s[pr] = att.astype(BF16)

        for pr in range(HG_HEADS // 2):
            cs2 = slice(2 * pr * HG_DK, 2 * (pr + 1) * HG_DK)
            q2 = q_s[rows, cs2]
            k2 = k_s[rows, cs2]
            v2 = v_s[rows, cs2]
            qg = (q2 * dec_s[0:c, cs2]).astype(BF16)
            st_t = jnp.concatenate(
                [jnp.concatenate([st_s[2 * pr].astype(BF16), zblk], axis=1),
                 jnp.concatenate([zblk, st_s[2 * pr + 1].astype(BF16)], axis=1)], axis=0)
            o2 = lax.dot_general(qg, st_t, NT_DIMS, preferred_element_type=F32)
            o2 = o2 + _dot(att_s[pr], blockdiag(v2, zrow))
            qk = q2 * k2
            for i in range(2):
                hd = 2 * pr + i
                cs = slice(hd * HG_DK, (hd + 1) * HG_DK)
                hs = slice(i * HG_DK, (i + 1) * HG_DK)
                o = o2[:, hs] + jnp.sum(qk[:, hs], axis=-1, keepdims=True) * v2[:, hs].astype(F32)
                o = o * lax.rsqrt(jnp.mean(o * o, axis=-1, keepdims=True) + EPS) * hgn_ref[:, cs]
                y_s[rows, cs] = (o * gz_s[rows, cs]).astype(BF16)

        for pr in range(HG_HEADS // 2):
            cs2 = slice(2 * pr * HG_DK, 2 * (pr + 1) * HG_DK)
            khat = (k_s[rows, cs2] * dec_s[c:2 * c, cs2]).astype(BF16)
            upd = lax.dot_general(v_s[rows, cs2], khat, TN_DIMS, preferred_element_type=F32)
            last = dec_s[c - 1:c, cs2]
            for i in range(2):
                hs = slice(i * HG_DK, (i + 1) * HG_DK)
                st_s[2 * pr + i] = st_s[2 * pr + i] * last[:, hs] + upd[hs, hs]
        return carry

    lax.fori_loop(0, tt // c, chunk, 0)
    yb_ref[0] = _dot(y_s[...], wo_ref[...])

    @pl.when(t == nt - 1)
    def _():
        for hd in range(HG_HEADS):
            nh_ref[0, hd] = st_s[hd].T


def _mixer_b_prompt(x, norm_g, w, lb_raw, hgn, wo, prev, layer, depth, tt):
    bsz, seq, d = x.shape
    nt = seq // tt
    seg3, masks, levels = _hgrn_chunk_consts(HG_CHUNK)
    w_bk = HG_HEADS * HG_DK
    kern = functools.partial(_mixer_b_kernel, tt=tt, nt=nt, layer=layer, levels=levels)
    operands = [x, norm_g, w, lb_raw, hgn, wo, seg3, masks]
    slot_specs, slot_args, aliases = _stacked_slot(prev, len(operands), 1)
    return pl.pallas_call(
        kern,
        grid=(bsz, nt),
        in_specs=[
            pl.BlockSpec((1, tt, d), lambda b, t: (b, t, 0)),
            _full((1, d)), _full(w.shape), _full(lb_raw.shape), _full((1, w_bk)), _full(wo.shape),
            _full(seg3.shape), _full(masks.shape),
        ] + slot_specs,
        out_specs=[
            pl.BlockSpec((1, tt, d), lambda b, t: (b, t, 0)),
            pl.BlockSpec((None, 1, HG_HEADS, HG_DK, HG_DV), lambda b, t: (layer, b, 0, 0, 0)),
        ],
        out_shape=[
            jax.ShapeDtypeStruct((bsz, seq, d), F32),
            jax.ShapeDtypeStruct((depth, bsz, HG_HEADS, HG_DK, HG_DV), F32),
        ],
        input_output_aliases=aliases,
        scratch_shapes=[
            pltpu.VMEM((tt, w_bk), F32),
            pltpu.VMEM((tt, w_bk), F32),
            pltpu.VMEM((tt, w_bk), F32),
            pltpu.VMEM((tt, w_bk), BF16),
            pltpu.VMEM((tt, w_bk), F32),
            pltpu.VMEM((tt, w_bk), BF16),
            pltpu.VMEM(((levels + 2) * HG_CHUNK, w_bk), F32),
            pltpu.VMEM((HG_HEADS // 2, HG_CHUNK, 2 * HG_CHUNK), BF16),
            pltpu.VMEM((HG_HEADS, HG_DV, HG_DK), F32),
        ],
        compiler_params=_params(("arbitrary", "arbitrary")),
        name="mixer_b_prompt",
    )(*operands, *slot_args)


def _mixer_b_sample_proj_kernel(x_ref, g_ref, w_ref, lbraw_ref,
                                qg_ref, kh_ref, v_ref, eb_ref, oi_ref, gz_ref, *, steps, n, layer):
    w_bk = HG_HEADS * HG_DK
    h = _rms_rows(x_ref[...], g_ref[...]).astype(BF16)
    p = _dot(h, w_ref[...])
    logf, kk = _forget_terms(p[:, w_bk:2 * w_bk], _lower_bound(lbraw_ref, layer))
    q = p[:, :w_bk]
    v = p[:, 2 * w_bk:3 * w_bk]
    gz_ref[...] = _silu(p[:, 3 * w_bk:])
    v_ref[...] = v.astype(BF16)

    rs = lambda a, s: a[s * n:(s + 1) * n, :]
    bc = []
    for s in range(steps):
        bc.append(rs(logf, s) if s == 0 else bc[-1] + rs(logf, s))
    eb_ref[...] = jnp.exp(bc[-1])
    ones = jnp.ones((HG_DK, HG_DK), BF16)
    for s in range(steps):
        qg_ref[s * n:(s + 1) * n, :] = (rs(q, s) * jnp.exp(bc[s])).astype(BF16)
        kh_ref[s * n:(s + 1) * n, :] = (rs(kk, s) * jnp.exp(bc[-1] - bc[s])).astype(BF16)
        acc = jnp.zeros((n, w_bk), F32)
        for r in range(s + 1):
            pr = rs(q, s) * rs(kk, r)
            if r < s:
                pr = pr * jnp.exp(bc[s] - bc[r])
            pr = pr.astype(BF16)
            att = jnp.concatenate(
                [_dot(pr[:, hd * HG_DK:(hd + 1) * HG_DK], ones) for hd in range(HG_HEADS)], axis=1)
            acc = acc + att * rs(v, r)
        oi_ref[s * n:(s + 1) * n, :] = acc


def _mixer_b_sample_proj(x2, norm_g, w, lb_raw, layer, steps):
    rows, d = x2.shape
    n = rows // steps
    w_bk = HG_HEADS * HG_DK
    kern = functools.partial(_mixer_b_sample_proj_kernel, steps=steps, n=n, layer=layer)
    return pl.pallas_call(
        kern,
        grid=(1,),
        in_specs=[_full((rows, d)), _full((1, d)), _full(w.shape), _full(lb_raw.shape)],
        out_specs=[_full((rows, w_bk)), _full((rows, w_bk)), _full((rows, w_bk)), _full((n, w_bk)),
                   _full((rows, w_bk)), _full((rows, w_bk))],
        out_shape=[
            jax.ShapeDtypeStruct((rows, w_bk), BF16),
            jax.ShapeDtypeStruct((rows, w_bk), BF16),
            jax.ShapeDtypeStruct((rows, w_bk), BF16),
            jax.ShapeDtypeStruct((n, w_bk), F32),
            jax.ShapeDtypeStruct((rows, w_bk), F32),
            jax.ShapeDtypeStruct((rows, w_bk), F32),
        ],
        compiler_params=_params(("arbitrary",)),
        name="mixer_b_sample_proj",
    )(x2, norm_g, w, lb_raw)


def _mixer_b_sample_state_kernel(st_ref, qg_ref, kh_ref, v_ref, eb_ref, *rest, steps, nb):
    io_ref, ns_ref = rest[-2:]
    pad = SUBLANES - 3 - steps
    ones3 = jnp.ones((3, HG_DV), BF16)
    for i in range(nb):
        for hd in range(HG_HEADS):
            cs = slice(hd * HG_DK, (hd + 1) * HG_DK)
            s = st_ref[i, hd]
            io_ref[:, i, cs] = _dot(qg_ref[:, i, cs], s.astype(BF16))
            hi, mid, lo = _split3(eb_ref[i:i + 1, cs])
            lhs = jnp.concatenate([hi, mid, lo, kh_ref[:, i, cs], jnp.zeros((pad, HG_DK), BF16)], axis=0)
            top = jnp.concatenate([ones3, jnp.zeros((3, HG_DV), BF16)], axis=1)
            mid_rows = jnp.concatenate([jnp.zeros((steps, HG_DV), BF16), v_ref[:, i, cs]], axis=1)
            rhs = jnp.concatenate([top, mid_rows, jnp.zeros((pad, 2 * HG_DV), BF16)], axis=0)
            r = lax.dot_general(lhs, rhs, TN_DIMS, preferred_element_type=F32)
            ns_ref[i, hd] = r[:, :HG_DV] * s + r[:, HG_DV:]


def _mixer_b_sample_state(states, qg, kh, v, eb, prev, layer, nb):
    steps, n, w_bk = qg.shape
    kern = functools.partial(_mixer_b_sample_state_kernel, steps=steps, nb=nb)
    tok = pl.BlockSpec((steps, nb, w_bk), lambda i: (0, i, 0))
    st = pl.BlockSpec((None, nb, HG_HEADS, HG_DK, HG_DV), lambda i: (layer, i, 0, 0, 0))
    operands = [states, qg, kh, v, eb]
    slot_specs, slot_args, aliases = _stacked_slot(prev, len(operands), 1)
    return pl.pallas_call(
        kern,
        grid=(n // nb,),
        in_specs=[st, tok, tok, tok, pl.BlockSpec((nb, w_bk), lambda i: (i, 0))] + slot_specs,
        out_specs=[tok, st],
        out_shape=[jax.ShapeDtypeStruct((steps, n, w_bk), F32), jax.ShapeDtypeStruct(states.shape, F32)],
        input_output_aliases=aliases,
        compiler_params=_params(("arbitrary",)),
        name="mixer_b_sample_state",
    )(*operands, *slot_args)


def _mixer_b_sample_out_kernel(io_ref, oi_ref, gz_ref, hgn_ref, wo_ref, yb_ref):
    o = io_ref[...] + oi_ref[...]
    parts = []
    for hd in range(HG_HEADS):
        oh = o[:, hd * HG_DV:(hd + 1) * HG_DV]
        parts.append(oh * lax.rsqrt(jnp.mean(oh * oh, axis=-1, keepdims=True) + EPS))
    y = jnp.concatenate(parts, axis=1) * hgn_ref[...] * gz_ref[...]
    yb_ref[...] = _dot(y.astype(BF16), wo_ref[...])


def _mixer_b_sample_out(inter, intra, gz, hgn, wo):
    rows, w_bv = inter.shape
    return pl.pallas_call(
        _mixer_b_sample_out_kernel,
        grid=(1,),
        in_specs=[_full((rows, w_bv))] * 3 + [_full((1, w_bv)), _full(wo.shape)],
        out_specs=_full((rows, wo.shape[1])),
        out_shape=jax.ShapeDtypeStruct((rows, wo.shape[1]), F32),
        compiler_params=_params(("arbitrary",)),
        name="mixer_b_sample_out",
    )(inter, intra, gz, hgn, wo)


def kernel(x_prompt, x_sample, state_conv_a, state_hgrn, state_conv_c, norm_g, w_in, gate_b, conv_a_w,
           conv_a_b, ln_g, ln_b, w_a_out, lb_raw, hg_norm_g, w_b_out, conv_c_w, w_c_out, w_o, final_norm_g):
    depth = w_in.shape[0]
    d = x_prompt.shape[-1]
    n_seq, steps, _ = x_sample.shape
    w_bk = HG_HEADS * HG_DK
    w_bv = HG_HEADS * HG_DV
    col_a = 3 * d
    col_b = col_a + 2 * w_bk + 2 * w_bv
    col_c = col_b + 4 * d
    row = lambda a: a.reshape(1, -1)
    fin_g = row(final_norm_g)

    xp = x_prompt
    xs = jnp.swapaxes(x_sample, 0, 1)
    pa, pc, sc = [], [], []
    ph = sa = sh = None
    for l in range(depth):
        cols = lambda a, b: w_in[l, :, a:b].astype(BF16)
        w_a, w_b, w_c, w_g = cols(0, col_a), cols(col_a, col_b), cols(col_b, col_c), cols(col_c, None)
        wao, wbo, wco, woo = (w_a_out[l].astype(BF16), w_b_out[l].astype(BF16), w_c_out[l].astype(BF16),
                              w_o[l].astype(BF16))
        ng, gb = row(norm_g[l]), row(gate_b[l])
        cab, lng, lnb, hgn = row(conv_a_b[l]), row(ln_g[l]), row(ln_b[l]), row(hg_norm_g[l])
        final = l == depth - 1

        ya, na = _mixer_a_prompt(xp, ng, w_a, conv_a_w[l], cab, lng, lnb, wao, tt=512)
        yb, ph = _mixer_b_prompt(xp, ng, w_b, lb_raw, hgn, wbo, ph, layer=l, depth=depth, tt=256)
        yc, nc = _mixer_c_prompt(xp, ng, w_c, conv_c_w[l], wco, tt=512)
        xp = _merge(xp, ya, yb, yc, ng, w_g, gb, woo, fin_g, tt=512, final=final)
        pa.append(na); pc.append(nc)

        xs2 = xs.reshape(steps * n_seq, d)
        ya, sa = _mixer_a_sample(xs, state_conv_a, ng, w_a, conv_a_w[l], cab, lng, lnb, wao, sa, layer=l, nb=32)
        qg, kh, v, eb, intra, gz = _mixer_b_sample_proj(xs2, ng, w_b, lb_raw, layer=l, steps=steps)
        tm = lambda a: a.reshape(steps, n_seq, w_bk)
        inter, sh = _mixer_b_sample_state(state_hgrn, tm(qg), tm(kh), tm(v), eb, sh, layer=l, nb=8)
        yb = _mixer_b_sample_out(inter.reshape(steps * n_seq, w_bv), intra, gz, hgn, wbo)
        yc, nc = _mixer_c_sample(xs2, state_conv_c[l], ng, w_c, conv_c_w[l], wco, steps=steps)
        tile = lambda a: a.reshape(1, steps * n_seq, d)
        xs = _merge(tile(xs), tile(ya), tile(yb), tile(yc), ng, w_g, gb, woo, fin_g,
                    tt=steps * n_seq, final=final).reshape(steps, n_seq, d)
        sc.append(nc)

    return (xp, jnp.swapaxes(xs, 0, 1), jnp.stack(pa), ph, jnp.stack(pc), sa, sh, jnp.stack(sc))
```

```python
import functools

import numpy as np
import jax
import jax.numpy as jnp
from jax import lax
from jax.experimental import pallas as pl
from jax.experimental.pallas import tpu as pltpu

F32 = jnp.float32
BF16 = jnp.bfloat16

EPS = 1e-6
D_MODEL = 1024
HG_HEADS = 8
HG_DK = 128
HG_DV = 128
CONV_A = 31
CONV_C = 3
N_BRANCH = 3

SUBLANES = 8
LANES = 128
MXU_COLS = 256
HALO_A = 32
HALO_C = 8
CONV_ROWS = 64
HG_CHUNK = 64
VMEM_LIMIT = 56 * 1024 * 1024

NT_DIMS = (((1,), (1,)), ((), ()))
TN_DIMS = (((0,), (0,)), ((), ()))


def _sigmoid(x):
    return 1.0 / (1.0 + jnp.exp(-x))


def _silu(x):
    return x * _sigmoid(x)


def _rms_rows(x, g):
    ms = jnp.mean(x * x, axis=-1, keepdims=True)
    return x * lax.rsqrt(ms + EPS) * g


def _layernorm_rows(x, g, b):
    mu = jnp.mean(x, axis=-1, keepdims=True)
    xc = x - mu
    var = jnp.mean(xc * xc, axis=-1, keepdims=True)
    return xc * lax.rsqrt(var + EPS) * g + b


def _dot(a, b):
    return jnp.dot(a, b, preferred_element_type=F32)


def _params(sem):
    return pltpu.CompilerParams(dimension_semantics=sem, vmem_limit_bytes=VMEM_LIMIT,
                                )


def _full(shape):
    n = len(shape)
    return pl.BlockSpec(shape, lambda *_: (0,) * n)


def _stacked_slot(prev, n_inputs, out_index):
    if prev is None:
        return [], [], {}
    return [pl.BlockSpec(memory_space=pl.ANY)], [prev], {n_inputs: out_index}


def _mixer_a_kernel(h_ref, w_ref, cw_ref, cb_ref, lng_ref, lnb_ref, wo_ref,
                    ya_ref, na_ref, ubuf, cabuf, shbuf, *, tt, nt):
    t = pl.program_id(1)

    @pl.when(t == 0)
    def _():
        ubuf[0:HALO_A, :] = jnp.zeros((HALO_A, D_MODEL), F32)

    h = h_ref[0]
    gate = _sigmoid(_dot(h, w_ref[:, D_MODEL:2 * D_MODEL]))
    ubuf[HALO_A:HALO_A + tt, :] = _dot(h, w_ref[:, :D_MODEL]) * gate
    gz = _silu(_dot(h, w_ref[:, 2 * D_MODEL:]))

    off = HALO_A - (CONV_A - 1)

    nsh = HALO_A + tt - SUBLANES

    def lane_tile(c, carry):
        cs = pl.ds(pl.multiple_of(c * LANES, LANES), LANES)
        for r in range(1, SUBLANES):
            shbuf[r - 1, :, cs] = ubuf[r:r + nsh, cs]
        for rb in range(tt // CONV_ROWS):
            r0 = rb * CONV_ROWS
            acc = jnp.broadcast_to(cb_ref[:, cs], (CONV_ROWS, LANES))
            for j in range(CONV_A):
                base, r = divmod(off + j, SUBLANES)
                lo = r0 + base * SUBLANES
                win = ubuf[lo:lo + CONV_ROWS, cs] if r == 0 else shbuf[r - 1, lo:lo + CONV_ROWS, cs]
                acc = acc + cw_ref[j:j + 1, cs] * win
            cabuf[r0:r0 + CONV_ROWS, cs] = acc
        return carry

    lax.fori_loop(0, D_MODEL // LANES, lane_tile, 0)

    @pl.when(t == nt - 1)
    def _():
        na_ref[0] = ubuf[tt + off:tt + HALO_A, :]

    ubuf[0:HALO_A, :] = ubuf[tt:tt + HALO_A, :]

    y = _silu(_layernorm_rows(cabuf[...], lng_ref[...], lnb_ref[...])) * gz
    ya_ref[0] = _dot(y.astype(BF16), wo_ref[...])


def _mixer_a_prompt(h, w, cw, cb, lng, lnb, wo, tt):
    bsz, seq, d = h.shape
    nt = seq // tt
    kern = functools.partial(_mixer_a_kernel, tt=tt, nt=nt)
    return pl.pallas_call(
        kern,
        grid=(bsz, nt),
        in_specs=[
            pl.BlockSpec((1, tt, d), lambda b, t: (b, t, 0)),
            _full(w.shape), _full(cw.shape), _full((1, d)), _full((1, d)), _full((1, d)),
            _full(wo.shape),
        ],
        out_specs=[
            pl.BlockSpec((1, tt, d), lambda b, t: (b, t, 0)),
            pl.BlockSpec((1, CONV_A - 1, d), lambda b, t: (b, 0, 0)),
        ],
        out_shape=[
            jax.ShapeDtypeStruct((bsz, seq, d), F32),
            jax.ShapeDtypeStruct((bsz, CONV_A - 1, d), F32),
        ],
        scratch_shapes=[
            pltpu.VMEM((HALO_A + tt, d), F32),
            pltpu.VMEM((tt, d), F32),
            pltpu.VMEM((SUBLANES - 1, HALO_A + tt - SUBLANES, d), F32),
        ],
        compiler_params=_params(("arbitrary", "arbitrary")),
        name="mixer_a_prompt",
    )(h, w, cw, cb, lng, lnb, wo)


def _mixer_a_sample_kernel(h_ref, st_ref, w_ref, cw_ref, cb_ref, lng_ref, lnb_ref, wo_ref, *rest,
                           steps, nb):
    ya_ref, na_ref = rest[-2:]
    h = h_ref[...].reshape(steps * nb, D_MODEL)
    p = _dot(h, w_ref[...])
    u = p[:, :D_MODEL] * _sigmoid(p[:, D_MODEL:2 * D_MODEL])
    gz = _silu(p[:, 2 * D_MODEL:])
    past = CONV_A - 1

    acc = [jnp.broadcast_to(cb_ref[...], (nb, D_MODEL)) for _ in range(steps)]
    for i in range(past + steps):
        slab = st_ref[:, i, :] if i < past else u[(i - past) * nb:(i - past + 1) * nb, :]
        for s in range(steps):
            j = i - s
            if 0 <= j < CONV_A:
                acc[s] = acc[s] + cw_ref[j:j + 1, :] * slab
    ca = jnp.concatenate(acc, axis=0)

    na_ref[:, 0:past - steps, :] = st_ref[:, steps:past, :]
    for s in range(steps):
        na_ref[:, past - steps + s, :] = u[s * nb:(s + 1) * nb, :]

    y = _silu(_layernorm_rows(ca, lng_ref[...], lnb_ref[...])) * gz
    ya_ref[...] = _dot(y.astype(BF16), wo_ref[...]).reshape(steps, nb, D_MODEL)


def _mixer_a_sample(h_tm, states, w, cw, cb, lng, lnb, wo, prev, layer, nb):
    steps, n, d = h_tm.shape
    kern = functools.partial(_mixer_a_sample_kernel, steps=steps, nb=nb)
    st = pl.BlockSpec((None, nb, CONV_A - 1, d), lambda i: (layer, i, 0, 0))
    operands = [h_tm, states, w, cw, cb, lng, lnb, wo]
    slot_specs, slot_args, aliases = _stacked_slot(prev, len(operands), 1)
    return pl.pallas_call(
        kern,
        grid=(n // nb,),
        in_specs=[
            pl.BlockSpec((steps, nb, d), lambda i: (0, i, 0)),
            st,
            _full(w.shape), _full(cw.shape), _full((1, d)), _full((1, d)), _full((1, d)),
            _full(wo.shape),
        ] + slot_specs,
        out_specs=[pl.BlockSpec((steps, nb, d), lambda i: (0, i, 0)), st],
        out_shape=[jax.ShapeDtypeStruct((steps, n, d), F32), jax.ShapeDtypeStruct(states.shape, F32)],
        input_output_aliases=aliases,
        compiler_params=_params(("arbitrary",)),
        name="mixer_a_sample",
    )(*operands, *slot_args)


def _mixer_c_kernel(h_ref, w_ref, cw_ref, wo_ref, yc_ref, nc_ref, vbuf, *, tt, nt):
    t = pl.program_id(1)

    @pl.when(t == 0)
    def _():
        vbuf[0:HALO_C, :] = jnp.zeros((HALO_C, D_MODEL), F32)

    h = h_ref[0]
    vbuf[HALO_C:HALO_C + tt, :] = (_dot(h, w_ref[:, D_MODEL:2 * D_MODEL])
                                   * _dot(h, w_ref[:, 2 * D_MODEL:3 * D_MODEL]))
    off = HALO_C - (CONV_C - 1)
    cc = cw_ref[0:1, :] * vbuf[off:off + tt, :]
    for j in range(1, CONV_C):
        cc = cc + cw_ref[j:j + 1, :] * vbuf[off + j:off + j + tt, :]

    @pl.when(t == nt - 1)
    def _():
        nc_ref[0] = vbuf[tt + off:tt + HALO_C, :]

    vbuf[0:HALO_C, :] = vbuf[tt:tt + HALO_C, :]
    y = _dot(h, w_ref[:, :D_MODEL]) * cc * _silu(_dot(h, w_ref[:, 3 * D_MODEL:]))
    yc_ref[0] = _dot(y.astype(BF16), wo_ref[...])


def _mixer_c_prompt(h, w, cw, wo, tt):
    bsz, seq, d = h.shape
    nt = seq // tt
    kern = functools.partial(_mixer_c_kernel, tt=tt, nt=nt)
    return pl.pallas_call(
        kern,
        grid=(bsz, nt),
        in_specs=[
            pl.BlockSpec((1, tt, d), lambda b, t: (b, t, 0)),
            _full(w.shape), _full(cw.shape), _full(wo.shape),
        ],
        out_specs=[
            pl.BlockSpec((1, tt, d), lambda b, t: (b, t, 0)),
            pl.BlockSpec((1, CONV_C - 1, d), lambda b, t: (b, 0, 0)),
        ],
        out_shape=[
            jax.ShapeDtypeStruct((bsz, seq, d), F32),
            jax.ShapeDtypeStruct((bsz, CONV_C - 1, d), F32),
        ],
        scratch_shapes=[pltpu.VMEM((HALO_C + tt, d), F32)],
        compiler_params=_params(("arbitrary", "arbitrary")),
        name="mixer_c_prompt",
    )(h, w, cw, wo)


def _mixer_c_sample_kernel(h_ref, st_ref, w_ref, cw_ref, wo_ref, yc_ref, nc_ref, *, steps, n):
    h = h_ref[...]
    p = _dot(h, w_ref[...])
    v = p[:, D_MODEL:2 * D_MODEL] * p[:, 2 * D_MODEL:3 * D_MODEL]
    past = CONV_C - 1
    slabs = [st_ref[:, i, :] for i in range(past)] + [v[s * n:(s + 1) * n, :] for s in range(steps)]
    rows = []
    for s in range(steps):
        cc = cw_ref[0:1, :] * slabs[s]
        for j in range(1, CONV_C):
            cc = cc + cw_ref[j:j + 1, :] * slabs[s + j]
        rows.append(cc)
    cc = jnp.concatenate(rows, axis=0)
    for i in range(past):
        nc_ref[:, i, :] = slabs[steps + i]
    y = p[:, :D_MODEL] * cc * _silu(p[:, 3 * D_MODEL:])
    yc_ref[...] = _dot(y.astype(BF16), wo_ref[...])


def _mixer_c_sample(h2, state, w, cw, wo, steps):
    rows, d = h2.shape
    n = rows // steps
    kern = functools.partial(_mixer_c_sample_kernel, steps=steps, n=n)
    return pl.pallas_call(
        kern,
        grid=(1,),
        in_specs=[_full((rows, d)), _full(state.shape), _full(w.shape), _full(cw.shape), _full(wo.shape)],
        out_specs=[_full((rows, d)), _full(state.shape)],
        out_shape=[jax.ShapeDtypeStruct((rows, d), F32), jax.ShapeDtypeStruct(state.shape, F32)],
        compiler_params=_params(("arbitrary",)),
        name="mixer_c_sample",
    )(h2, state, w, cw, wo)


def _merge_kernel(x_ref, h_ref, ya_ref, yb_ref, yc_ref, w_ref, gb_ref, wo_ref, ng_ref, *out_refs, final):
    h = h_ref[0]
    m = None
    for i, y_ref in enumerate((ya_ref, yb_ref, yc_ref)):
        cs = slice(i * D_MODEL, (i + 1) * D_MODEL)
        term = _sigmoid(_dot(h, w_ref[:, cs]) + gb_ref[:, cs]) * y_ref[0]
        m = term if m is None else m + term
    out = x_ref[0] + _dot(m.astype(BF16), wo_ref[...])
    normed = _rms_rows(out, ng_ref[...])
    if final:
        out_refs[0][0] = normed
    else:
        out_refs[0][0] = out
        out_refs[1][0] = normed.astype(BF16)


def _merge(x, h, ya, yb, yc, w, gate_b, wo, next_g, tt, final):
    bsz, seq, d = x.shape
    tile = pl.BlockSpec((1, tt, d), lambda b, t: (b, t, 0))
    kern = functools.partial(_merge_kernel, final=final)
    out_shape = [jax.ShapeDtypeStruct((bsz, seq, d), F32)]
    if not final:
        out_shape.append(jax.ShapeDtypeStruct((bsz, seq, d), BF16))
    outs = pl.pallas_call(
        kern,
        grid=(bsz, seq // tt),
        in_specs=[tile, tile, tile, tile, tile, _full(w.shape), _full(gate_b.shape), _full(wo.shape),
                  _full((1, d))],
        out_specs=[tile] * len(out_shape),
        out_shape=out_shape,
        compiler_params=_params(("arbitrary", "arbitrary")),
        name="merge",
    )(x, h, ya, yb, yc, w, gate_b, wo, next_g)
    return (outs[0], None) if final else (outs[0], outs[1])


def _rmsnorm_kernel(x_ref, g_ref, h_ref):
    h_ref[0] = _rms_rows(x_ref[0], g_ref[...]).astype(BF16)


def _rmsnorm_bf16(x, g, tt):
    bsz, seq, d = x.shape
    tile = pl.BlockSpec((1, tt, d), lambda b, t: (b, t, 0))
    return pl.pallas_call(
        _rmsnorm_kernel,
        grid=(bsz, seq // tt),
        in_specs=[tile, _full((1, d))],
        out_specs=tile,
        out_shape=jax.ShapeDtypeStruct((bsz, seq, d), BF16),
        compiler_params=_params(("arbitrary", "arbitrary")),
        name="rmsnorm_bf16",
    )(x, g)


def _lower_bound(lbraw_ref, layer):
    depth = lbraw_ref.shape[0]
    rows = [lbraw_ref[i:i + 1, :] for i in range(depth)]
    mx = functools.reduce(jnp.maximum, rows)
    ex = [jnp.exp(r - mx) for r in rows]
    tot = functools.reduce(lambda a, b: a + b, ex)
    csum = []
    run = None
    for e in ex:
        sm = e / tot
        run = sm if run is None else run + sm
        csum.append(run)
    return csum[layer] - csum[0]


def _forget_terms(f_pre, lb):
    e = jnp.exp(-jnp.abs(f_pre))
    r = 1.0 / (1.0 + e)
    er = e * r
    pos = f_pre > 0.0
    log_sig = jnp.minimum(f_pre, 0.0) + jnp.log(r)
    one_m_lb = 1.0 - lb
    f = lb + one_m_lb * jnp.where(pos, r, er)
    logf = jnp.where(lb > 0.0, jnp.log(f), log_sig)
    k = one_m_lb * jnp.where(pos, er, r)
    return logf, k


def _split3(x):
    hi = x.astype(BF16)
    r1 = x - hi.astype(F32)
    mid = r1.astype(BF16)
    lo = (r1 - mid.astype(F32)).astype(BF16)
    return hi, mid, lo


def _hgrn_chunk_consts(c):
    levels = c.bit_length() - 1
    t = np.arange(c)
    col = t[None, :]
    row = t[:, None]
    blocks = [col <= row, col > row]
    masks = []
    for j in range(levels):
        half = 1 << j
        p = ((t >> (j + 1)) << (j + 1)) + half
        tgt = ((t >> j) & 1) == 1
        seg_t = (col >= p[:, None]) & (col <= row)
        seg_s = (col > row) & (col <= p[:, None] - 1)
        blocks.append(np.where(tgt[:, None], seg_t, seg_s))
        same = (row >> (j + 1)) == (col >> (j + 1))
        masks.append(same & tgt[:, None] & (~tgt)[None, :])
    assert (np.sum(np.stack(masks), axis=0) == (col < row)).all()
    seg = np.concatenate(blocks, axis=0).astype(np.float32)
    seg3 = np.concatenate([seg, seg, seg], axis=1)
    pair_masks = np.tile(np.stack(masks).astype(np.float32), (1, 1, 2))
    return jnp.asarray(seg3, BF16), jnp.asarray(pair_masks), levels


def _mixer_b_kernel(h_ref, w_ref, lbraw_ref, hgn_ref, wo_ref, seg_ref, mask_ref, *rest,
                    tt, nt, layer, levels):
    yb_ref, nh_ref, q_s, k_s, lf_s, v_s, gz_s, y_s, dec2_s, att2_s, st_s = rest[-11:]
    t = pl.program_id(1)
    c = HG_CHUNK
    w_bk = HG_HEADS * HG_DK

    @pl.when(t == 0)
    def _():
        st_s[...] = jnp.zeros(st_s.shape, F32)

    h = h_ref[0]
    logf, kk = _forget_terms(_dot(h, w_ref[:, w_bk:2 * w_bk]), _lower_bound(lbraw_ref, layer))
    lf_s[...] = logf
    k_s[...] = kk
    gz_s[...] = _silu(_dot(h, w_ref[:, 3 * w_bk:]))
    q_s[...] = _dot(h, w_ref[:, :w_bk])
    v_s[...] = _dot(h, w_ref[:, 2 * w_bk:3 * w_bk]).astype(BF16)

    def chunk(ci, dec_s, att_s):
        rows = pl.ds(pl.multiple_of(ci * c, c), c)
        hi, mid, lo = _split3(lf_s[rows, :])
        expo = _dot(seg_ref[...], jnp.concatenate([hi, mid, lo], axis=0))
        dec_s[...] = jnp.exp(expo)
        zrow = jnp.zeros((c, HG_DK), BF16)
        zblk = jnp.zeros((HG_DV, HG_DK), BF16)

        def blockdiag(m, z):
            return jnp.concatenate([jnp.concatenate([m[:, :HG_DK], z], axis=1),
                                    jnp.concatenate([z, m[:, HG_DK:]], axis=1)], axis=0)

        for pr in range(HG_HEADS // 2):
            cs2 = slice(2 * pr * HG_DK, 2 * (pr + 1) * HG_DK)
            q2 = q_s[rows, cs2]
            k2 = k_s[rows, cs2]
            att = None
            for j in range(levels):
                gj = dec_s[(2 + j) * c:(3 + j) * c, cs2]
                kj = (k2 * gj).astype(BF16)
                pj = lax.dot_general((q2 * gj).astype(BF16), blockdiag(kj, zrow), NT_DIMS,
                                     preferred_element_type=F32) * mask_ref[j]
                att = pj if att is None else att + pj
            att_s[pr] = att.astype(BF16)

        for pr in range(HG_HEADS // 2):
            cs2 = slice(2 * pr * HG_DK, 2 * (pr + 1) * HG_DK)
            q2 = q_s[rows, cs2]
            k2 = k_s[rows, cs2]
            v2 = v_s[rows, cs2]
            qg = (q2 * dec_s[0:c, cs2]).astype(BF16)
            st_t = jnp.concatenate(
                [jnp.concatenate([st_s[2 * pr].astype(BF16), zblk], axis=1),
                 jnp.concatenate([zblk, st_s[2 * pr + 1].astype(BF16)], axis=1)], axis=0)
            o2 = lax.dot_general(qg, st_t, NT_DIMS, preferred_element_type=F32)
            o2 = o2 + _dot(att_s[pr], blockdiag(v2, zrow))
            qk = q2 * k2
            for i in range(2):
                hd = 2 * pr + i
                cs = slice(hd * HG_DK, (hd + 1) * HG_DK)
                hs = slice(i * HG_DK, (i + 1) * HG_DK)
                o = o2[:, hs] + jnp.sum(qk[:, hs], axis=-1, keepdims=True) * v2[:, hs].astype(F32)
                o = o * lax.rsqrt(jnp.mean(o * o, axis=-1, keepdims=True) + EPS) * hgn_ref[:, cs]
                y_s[rows, cs] = (o * gz_s[rows, cs]).astype(BF16)

        for pr in range(HG_HEADS // 2):
            cs2 = slice(2 * pr * HG_DK, 2 * (pr + 1) * HG_DK)
            khat = (k_s[rows, cs2] * dec_s[c:2 * c, cs2]).astype(BF16)
            upd = lax.dot_general(v_s[rows, cs2], khat, TN_DIMS, preferred_element_type=F32)
            last = dec_s[c - 1:c, cs2]
            for i in range(2):
                hs = slice(i * HG_DK, (i + 1) * HG_DK)
                st_s[2 * pr + i] = st_s[2 * pr + i] * last[:, hs] + upd[hs, hs]

    def chunk_pair(cp, carry):
        for e in range(2):
            chunk(2 * cp + e, dec2_s.at[e], att2_s.at[e])
        return carry

    lax.fori_loop(0, tt // (2 * c), chunk_pair, 0)
    yb_ref[0] = _dot(y_s[...], wo_ref[...])

    @pl.when(t == nt - 1)
    def _():
        for hd in range(HG_HEADS):
            nh_ref[0, hd] = st_s[hd].T


def _mixer_b_prompt(h, w, lb_raw, hgn, wo, prev, layer, depth, tt):
    bsz, seq, d = h.shape
    nt = seq // tt
    seg3, masks, levels = _hgrn_chunk_consts(HG_CHUNK)
    w_bk = HG_HEADS * HG_DK
    kern = functools.partial(_mixer_b_kernel, tt=tt, nt=nt, layer=layer, levels=levels)
    operands = [h, w, lb_raw, hgn, wo, seg3, masks]
    slot_specs, slot_args, aliases = _stacked_slot(prev, len(operands), 1)
    return pl.pallas_call(
        kern,
        grid=(bsz, nt),
        in_specs=[
            pl.BlockSpec((1, tt, d), lambda b, t: (b, t, 0)),
            _full(w.shape), _full(lb_raw.shape), _full((1, w_bk)), _full(wo.shape),
            _full(seg3.shape), _full(masks.shape),
        ] + slot_specs,
        out_specs=[
            pl.BlockSpec((1, tt, d), lambda b, t: (b, t, 0)),
            pl.BlockSpec((None, 1, HG_HEADS, HG_DK, HG_DV), lambda b, t: (layer, b, 0, 0, 0)),
        ],
        out_shape=[
            jax.ShapeDtypeStruct((bsz, seq, d), F32),
            jax.ShapeDtypeStruct((depth, bsz, HG_HEADS, HG_DK, HG_DV), F32),
        ],
        input_output_aliases=aliases,
        scratch_shapes=[
            pltpu.VMEM((tt, w_bk), F32),
            pltpu.VMEM((tt, w_bk), F32),
            pltpu.VMEM((tt, w_bk), F32),
            pltpu.VMEM((tt, w_bk), BF16),
            pltpu.VMEM((tt, w_bk), F32),
            pltpu.VMEM((tt, w_bk), BF16),
            pltpu.VMEM((2, (levels + 2) * HG_CHUNK, w_bk), F32),
            pltpu.VMEM((2, HG_HEADS // 2, HG_CHUNK, 2 * HG_CHUNK), BF16),
            pltpu.VMEM((HG_HEADS, HG_DV, HG_DK), F32),
        ],
        compiler_params=_params(("arbitrary", "arbitrary")),
        name="mixer_b_prompt",
    )(*operands, *slot_args)


def _mixer_b_sample_proj_kernel(h_ref, w_ref, lbraw_ref,
                                qg_ref, kh_ref, v_ref, eb_ref, oi_ref, gz_ref, *, steps, n, layer):
    w_bk = HG_HEADS * HG_DK
    h = h_ref[...]
    p = _dot(h, w_ref[...])
    logf, kk = _forget_terms(p[:, w_bk:2 * w_bk], _lower_bound(lbraw_ref, layer))
    q = p[:, :w_bk]
    v = p[:, 2 * w_bk:3 * w_bk]
    gz_ref[...] = _silu(p[:, 3 * w_bk:])
    v_ref[...] = v.astype(BF16)

    rs = lambda a, s: a[s * n:(s + 1) * n, :]
    bc = []
    for s in range(steps):
        bc.append(rs(logf, s) if s == 0 else bc[-1] + rs(logf, s))
    eb_ref[...] = jnp.exp(bc[-1])
    ones = jnp.ones((HG_DK, HG_DK), BF16)
    for s in range(steps):
        qg_ref[s * n:(s + 1) * n, :] = (rs(q, s) * jnp.exp(bc[s])).astype(BF16)
        kh_ref[s * n:(s + 1) * n, :] = (rs(kk, s) * jnp.exp(bc[-1] - bc[s])).astype(BF16)
        acc = jnp.zeros((n, w_bk), F32)
        for r in range(s + 1):
            pr = rs(q, s) * rs(kk, r)
            if r < s:
                pr = pr * jnp.exp(bc[s] - bc[r])
            pr = pr.astype(BF16)
            att = jnp.concatenate(
                [_dot(pr[:, hd * HG_DK:(hd + 1) * HG_DK], ones) for hd in range(HG_HEADS)], axis=1)
            acc = acc + att * rs(v, r)
        oi_ref[s * n:(s + 1) * n, :] = acc


def _mixer_b_sample_proj(h2, w, lb_raw, layer, steps):
    rows, d = h2.shape
    n = rows // steps
    w_bk = HG_HEADS * HG_DK
    kern = functools.partial(_mixer_b_sample_proj_kernel, steps=steps, n=n, layer=layer)
    return pl.pallas_call(
        kern,
        grid=(1,),
        in_specs=[_full((rows, d)), _full(w.shape), _full(lb_raw.shape)],
        out_specs=[_full((rows, w_bk)), _full((rows, w_bk)), _full((rows, w_bk)), _full((n, w_bk)),
                   _full((rows, w_bk)), _full((rows, w_bk))],
        out_shape=[
            jax.ShapeDtypeStruct((rows, w_bk), BF16),
            jax.ShapeDtypeStruct((rows, w_bk), BF16),
            jax.ShapeDtypeStruct((rows, w_bk), BF16),
            jax.ShapeDtypeStruct((n, w_bk), F32),
            jax.ShapeDtypeStruct((rows, w_bk), F32),
            jax.ShapeDtypeStruct((rows, w_bk), F32),
        ],
        compiler_params=_params(("arbitrary",)),
        name="mixer_b_sample_proj",
    )(h2, w, lb_raw)


def _mixer_b_sample_state_kernel(st_ref, qg_ref, kh_ref, v_ref, eb_ref, *rest, steps, nb):
    io_ref, ns_ref = rest[-2:]
    pad = SUBLANES - 3 - steps
    ones3 = jnp.ones((3, HG_DV), BF16)
    for i in range(nb):
        for hd in range(HG_HEADS):
            cs = slice(hd * HG_DK, (hd + 1) * HG_DK)
            s = st_ref[i, hd]
            io_ref[:, i, cs] = _dot(qg_ref[:, i, cs], s.astype(BF16))
            hi, mid, lo = _split3(eb_ref[i:i + 1, cs])
            lhs = jnp.concatenate([hi, mid, lo, kh_ref[:, i, cs], jnp.zeros((pad, HG_DK), BF16)], axis=0)
            top = jnp.concatenate([ones3, jnp.zeros((3, HG_DV), BF16)], axis=1)
            mid_rows = jnp.concatenate([jnp.zeros((steps, HG_DV), BF16), v_ref[:, i, cs]], axis=1)
            rhs = jnp.concatenate([top, mid_rows, jnp.zeros((pad, 2 * HG_DV), BF16)], axis=0)
            r = lax.dot_general(lhs, rhs, TN_DIMS, preferred_element_type=F32)
            ns_ref[i, hd] = r[:, :HG_DV] * s + r[:, HG_DV:]


def _mixer_b_sample_state(states, qg, kh, v, eb, prev, layer, nb):
    steps, n, w_bk = qg.shape
    kern = functools.partial(_mixer_b_sample_state_kernel, steps=steps, nb=nb)
    tok = pl.BlockSpec((steps, nb, w_bk), lambda i: (0, i, 0))
    st = pl.BlockSpec((None, nb, HG_HEADS, HG_DK, HG_DV), lambda i: (layer, i, 0, 0, 0))
    operands = [states, qg, kh, v, eb]
    slot_specs, slot_args, aliases = _stacked_slot(prev, len(operands), 1)
    return pl.pallas_call(
        kern,
        grid=(n // nb,),
        in_specs=[st, tok, tok, tok, pl.BlockSpec((nb, w_bk), lambda i: (i, 0))] + slot_specs,
        out_specs=[tok, st],
        out_shape=[jax.ShapeDtypeStruct((steps, n, w_bk), F32), jax.ShapeDtypeStruct(states.shape, F32)],
        input_output_aliases=aliases,
        compiler_params=_params(("arbitrary",)),
        name="mixer_b_sample_state",
    )(*operands, *slot_args)


def _mixer_b_sample_out_kernel(io_ref, oi_ref, gz_ref, hgn_ref, wo_ref, yb_ref):
    o = io_ref[...] + oi_ref[...]
    parts = []
    for hd in range(HG_HEADS):
        oh = o[:, hd * HG_DV:(hd + 1) * HG_DV]
        parts.append(oh * lax.rsqrt(jnp.mean(oh * oh, axis=-1, keepdims=True) + EPS))
    y = jnp.concatenate(parts, axis=1) * hgn_ref[...] * gz_ref[...]
    yb_ref[...] = _dot(y.astype(BF16), wo_ref[...])


def _mixer_b_sample_out(inter, intra, gz, hgn, wo):
    rows, w_bv = inter.shape
    return pl.pallas_call(
        _mixer_b_sample_out_kernel,
        grid=(1,),
        in_specs=[_full((rows, w_bv))] * 3 + [_full((1, w_bv)), _full(wo.shape)],
        out_specs=_full((rows, wo.shape[1])),
        out_shape=jax.ShapeDtypeStruct((rows, wo.shape[1]), F32),
        compiler_params=_params(("arbitrary",)),
        name="mixer_b_sample_out",
    )(inter, intra, gz, hgn, wo)


def kernel(x_prompt, x_sample, state_conv_a, state_hgrn, state_conv_c, norm_g, w_in, gate_b, conv_a_w,
           conv_a_b, ln_g, ln_b, w_a_out, lb_raw, hg_norm_g, w_b_out, conv_c_w, w_c_out, w_o, final_norm_g):
    depth = w_in.shape[0]
    d = x_prompt.shape[-1]
    n_seq, steps, _ = x_sample.shape
    w_bk = HG_HEADS * HG_DK
    w_bv = HG_HEADS * HG_DV
    col_a = 3 * d
    col_b = col_a + 2 * w_bk + 2 * w_bv
    col_c = col_b + 4 * d
    row = lambda a: a.reshape(1, -1)
    fin_g = row(final_norm_g)

    rows = steps * n_seq
    xp = x_prompt
    xs = jnp.swapaxes(x_sample, 0, 1).reshape(1, rows, d)
    hp = _rmsnorm_bf16(xp, row(norm_g[0]), tt=512)
    hs = _rmsnorm_bf16(xs, row(norm_g[0]), tt=rows)
    pa, pc, sc = [], [], []
    ph = sa = sh = None
    for l in range(depth):
        cols = lambda a, b: w_in[l, :, a:b].astype(BF16)
        w_a, w_b, w_c, w_g = cols(0, col_a), cols(col_a, col_b), cols(col_b, col_c), cols(col_c, None)
        wao, wbo, wco, woo = (w_a_out[l].astype(BF16), w_b_out[l].astype(BF16), w_c_out[l].astype(BF16),
                              w_o[l].astype(BF16))
        gb = row(gate_b[l])
        cab, lng, lnb, hgn = row(conv_a_b[l]), row(ln_g[l]), row(ln_b[l]), row(hg_norm_g[l])
        final = l == depth - 1
        next_g = fin_g if final else row(norm_g[l + 1])

        ya, na = _mixer_a_prompt(hp, w_a, conv_a_w[l], cab, lng, lnb, wao, tt=512)
        yb, ph = _mixer_b_prompt(hp, w_b, lb_raw, hgn, wbo, ph, layer=l, depth=depth, tt=512)
        yc, nc = _mixer_c_prompt(hp, w_c, conv_c_w[l], wco, tt=512)
        xp, hp = _merge(xp, hp, ya, yb, yc, w_g, gb, woo, next_g, tt=512, final=final)
        pa.append(na); pc.append(nc)

        hs2 = hs.reshape(rows, d)
        ya, sa = _mixer_a_sample(hs.reshape(steps, n_seq, d), state_conv_a, w_a, conv_a_w[l], cab, lng, lnb,
                                 wao, sa, layer=l, nb=32)
        qg, kh, v, eb, intra, gz = _mixer_b_sample_proj(hs2, w_b, lb_raw, layer=l, steps=steps)
        tm = lambda a: a.reshape(steps, n_seq, w_bk)
        inter, sh = _mixer_b_sample_state(state_hgrn, tm(qg), tm(kh), tm(v), eb, sh, layer=l, nb=8)
        yb = _mixer_b_sample_out(inter.reshape(rows, w_bv), intra, gz, hgn, wbo)
        yc, nc = _mixer_c_sample(hs2, state_conv_c[l], w_c, conv_c_w[l], wco, steps=steps)
        tile = lambda a: a.reshape(1, rows, d)
        xs, hs = _merge(xs, hs, tile(ya), tile(yb), tile(yc), w_g, gb, woo, next_g, tt=rows, final=final)
        sc.append(nc)

    y_sample = jnp.swapaxes(xs.reshape(steps, n_seq, d), 0, 1)
    return (xp, y_sample, jnp.stack(pa), ph, jnp.stack(pc), sa, sh, jnp.stack(sc))
```

```python
import functools

import numpy as np
import jax
import jax.numpy as jnp
from jax import lax
from jax.experimental import pallas as pl
from jax.experimental.pallas import tpu as pltpu

F32 = jnp.float32
BF16 = jnp.bfloat16

EPS = 1e-6
D_MODEL = 1024
HG_HEADS = 8
HG_DK = 128
HG_DV = 128
CONV_A = 31
CONV_C = 3
N_BRANCH = 3

SUBLANES = 8
LANES = 128
MXU_COLS = 256
HALO_A = 32
HALO_C = 8
CONV_ROWS = 64
HG_CHUNK = 64
HG_SLOTS = 4
PROMPT_TILE = 512
SAMPLE_CONV_SEQS = 32
SAMPLE_STATE_SEQS = 8
VMEM_LIMIT = 56 * 1024 * 1024

NT_DIMS = (((1,), (1,)), ((), ()))
TN_DIMS = (((0,), (0,)), ((), ()))


def _sigmoid(x):
    return 1.0 / (1.0 + jnp.exp(-x))


def _silu(x):
    return x * _sigmoid(x)


def _rms_rows(x, g):
    ms = jnp.mean(x * x, axis=-1, keepdims=True)
    return x * lax.rsqrt(ms + EPS) * g


def _layernorm_rows(x, g, b):
    mu = jnp.mean(x, axis=-1, keepdims=True)
    xc = x - mu
    var = jnp.mean(xc * xc, axis=-1, keepdims=True)
    return xc * lax.rsqrt(var + EPS) * g + b


def _dot(a, b):
    return jnp.dot(a, b, preferred_element_type=F32)


def _params(sem):
    return pltpu.CompilerParams(dimension_semantics=sem, vmem_limit_bytes=VMEM_LIMIT,
                                )


def _full(shape):
    n = len(shape)
    return pl.BlockSpec(shape, lambda *_: (0,) * n)


def _stacked_slot(prev, n_inputs, out_index):
    if prev is None:
        return [], [], {}
    return [pl.BlockSpec(memory_space=pl.ANY)], [prev], {n_inputs: out_index}


def _mixer_a_kernel(h_ref, w_ref, cw_ref, cb_ref, lng_ref, lnb_ref, wo_ref,
                    ya_ref, na_ref, ubuf, cabuf, shbuf, *, tt, nt):
    t = pl.program_id(1)

    @pl.when(t == 0)
    def _():
        ubuf[0:HALO_A, :] = jnp.zeros((HALO_A, D_MODEL), F32)

    h = h_ref[0]
    gate = _sigmoid(_dot(h, w_ref[:, D_MODEL:2 * D_MODEL]))
    ubuf[HALO_A:HALO_A + tt, :] = _dot(h, w_ref[:, :D_MODEL]) * gate
    gz = _silu(_dot(h, w_ref[:, 2 * D_MODEL:]))

    off = HALO_A - (CONV_A - 1)

    nsh = HALO_A + tt - SUBLANES

    def lane_tile(c, carry):
        cs = pl.ds(pl.multiple_of(c * LANES, LANES), LANES)
        for r in range(1, SUBLANES):
            shbuf[r - 1, :, cs] = ubuf[r:r + nsh, cs]
        for rb in range(tt // CONV_ROWS):
            r0 = rb * CONV_ROWS
            acc = jnp.broadcast_to(cb_ref[:, cs], (CONV_ROWS, LANES))
            for j in range(CONV_A):
                base, r = divmod(off + j, SUBLANES)
                lo = r0 + base * SUBLANES
                win = ubuf[lo:lo + CONV_ROWS, cs] if r == 0 else shbuf[r - 1, lo:lo + CONV_ROWS, cs]
                acc = acc + cw_ref[j:j + 1, cs] * win
            cabuf[r0:r0 + CONV_ROWS, cs] = acc
        return carry

    lax.fori_loop(0, D_MODEL // LANES, lane_tile, 0)

    @pl.when(t == nt - 1)
    def _():
        na_ref[0] = ubuf[tt + off:tt + HALO_A, :]

    ubuf[0:HALO_A, :] = ubuf[tt:tt + HALO_A, :]

    y = _silu(_layernorm_rows(cabuf[...], lng_ref[...], lnb_ref[...])) * gz
    ya_ref[0] = _dot(y.astype(BF16), wo_ref[...])


def _mixer_a_prompt(h, w, cw, cb, lng, lnb, wo, tt):
    bsz, seq, d = h.shape
    nt = seq // tt
    kern = functools.partial(_mixer_a_kernel, tt=tt, nt=nt)
    return pl.pallas_call(
        kern,
        grid=(bsz, nt),
        in_specs=[
            pl.BlockSpec((1, tt, d), lambda b, t: (b, t, 0)),
            _full(w.shape), _full(cw.shape), _full((1, d)), _full((1, d)), _full((1, d)),
            _full(wo.shape),
        ],
        out_specs=[
            pl.BlockSpec((1, tt, d), lambda b, t: (b, t, 0)),
            pl.BlockSpec((1, CONV_A - 1, d), lambda b, t: (b, 0, 0)),
        ],
        out_shape=[
            jax.ShapeDtypeStruct((bsz, seq, d), F32),
            jax.ShapeDtypeStruct((bsz, CONV_A - 1, d), F32),
        ],
        scratch_shapes=[
            pltpu.VMEM((HALO_A + tt, d), F32),
            pltpu.VMEM((tt, d), F32),
            pltpu.VMEM((SUBLANES - 1, HALO_A + tt - SUBLANES, d), F32),
        ],
        compiler_params=_params(("arbitrary", "arbitrary")),
        name="mixer_a_prompt",
    )(h, w, cw, cb, lng, lnb, wo)


def _mixer_a_sample_kernel(h_ref, st_ref, w_ref, cw_ref, cb_ref, lng_ref, lnb_ref, wo_ref, *rest,
                           steps, nb):
    ya_ref, na_ref = rest[-2:]
    h = h_ref[...].reshape(steps * nb, D_MODEL)
    p = _dot(h, w_ref[...])
    u = p[:, :D_MODEL] * _sigmoid(p[:, D_MODEL:2 * D_MODEL])
    gz = _silu(p[:, 2 * D_MODEL:])
    past = CONV_A - 1

    acc = [jnp.broadcast_to(cb_ref[...], (nb, D_MODEL)) for _ in range(steps)]
    for i in range(past + steps):
        slab = st_ref[:, i, :] if i < past else u[(i - past) * nb:(i - past + 1) * nb, :]
        for s in range(steps):
            j = i - s
            if 0 <= j < CONV_A:
                acc[s] = acc[s] + cw_ref[j:j + 1, :] * slab
    ca = jnp.concatenate(acc, axis=0)

    na_ref[:, 0:past - steps, :] = st_ref[:, steps:past, :]
    for s in range(steps):
        na_ref[:, past - steps + s, :] = u[s * nb:(s + 1) * nb, :]

    y = _silu(_layernorm_rows(ca, lng_ref[...], lnb_ref[...])) * gz
    ya_ref[...] = _dot(y.astype(BF16), wo_ref[...]).reshape(steps, nb, D_MODEL)


def _mixer_a_sample(h_tm, states, w, cw, cb, lng, lnb, wo, prev, layer, nb):
    steps, n, d = h_tm.shape
    kern = functools.partial(_mixer_a_sample_kernel, steps=steps, nb=nb)
    st = pl.BlockSpec((None, nb, CONV_A - 1, d), lambda i: (layer, i, 0, 0))
    operands = [h_tm, states, w, cw, cb, lng, lnb, wo]
    slot_specs, slot_args, aliases = _stacked_slot(prev, len(operands), 1)
    return pl.pallas_call(
        kern,
        grid=(n // nb,),
        in_specs=[
            pl.BlockSpec((steps, nb, d), lambda i: (0, i, 0)),
            st,
            _full(w.shape), _full(cw.shape), _full((1, d)), _full((1, d)), _full((1, d)),
            _full(wo.shape),
        ] + slot_specs,
        out_specs=[pl.BlockSpec((steps, nb, d), lambda i: (0, i, 0)), st],
        out_shape=[jax.ShapeDtypeStruct((steps, n, d), F32), jax.ShapeDtypeStruct(states.shape, F32)],
        input_output_aliases=aliases,
        compiler_params=_params(("arbitrary",)),
        name="mixer_a_sample",
    )(*operands, *slot_args)


def _mixer_c_kernel(h_ref, w_ref, cw_ref, wo_ref, yc_ref, nc_ref, vbuf, *, tt, nt):
    t = pl.program_id(1)

    @pl.when(t == 0)
    def _():
        vbuf[0:HALO_C, :] = jnp.zeros((HALO_C, D_MODEL), F32)

    h = h_ref[0]
    vbuf[HALO_C:HALO_C + tt, :] = (_dot(h, w_ref[:, D_MODEL:2 * D_MODEL])
                                   * _dot(h, w_ref[:, 2 * D_MODEL:3 * D_MODEL]))
    off = HALO_C - (CONV_C - 1)
    cc = cw_ref[0:1, :] * vbuf[off:off + tt, :]
    for j in range(1, CONV_C):
        cc = cc + cw_ref[j:j + 1, :] * vbuf[off + j:off + j + tt, :]

    @pl.when(t == nt - 1)
    def _():
        nc_ref[0] = vbuf[tt + off:tt + HALO_C, :]

    vbuf[0:HALO_C, :] = vbuf[tt:tt + HALO_C, :]
    y = _dot(h, w_ref[:, :D_MODEL]) * cc * _silu(_dot(h, w_ref[:, 3 * D_MODEL:]))
    yc_ref[0] = _dot(y.astype(BF16), wo_ref[...])


def _mixer_c_prompt(h, w, cw, wo, tt):
    bsz, seq, d = h.shape
    nt = seq // tt
    kern = functools.partial(_mixer_c_kernel, tt=tt, nt=nt)
    return pl.pallas_call(
        kern,
        grid=(bsz, nt),
        in_specs=[
            pl.BlockSpec((1, tt, d), lambda b, t: (b, t, 0)),
            _full(w.shape), _full(cw.shape), _full(wo.shape),
        ],
        out_specs=[
            pl.BlockSpec((1, tt, d), lambda b, t: (b, t, 0)),
            pl.BlockSpec((1, CONV_C - 1, d), lambda b, t: (b, 0, 0)),
        ],
        out_shape=[
            jax.ShapeDtypeStruct((bsz, seq, d), F32),
            jax.ShapeDtypeStruct((bsz, CONV_C - 1, d), F32),
        ],
        scratch_shapes=[pltpu.VMEM((HALO_C + tt, d), F32)],
        compiler_params=_params(("arbitrary", "arbitrary")),
        name="mixer_c_prompt",
    )(h, w, cw, wo)


def _mixer_c_sample_kernel(h_ref, st_ref, w_ref, cw_ref, wo_ref, yc_ref, nc_ref, *, steps, n):
    h = h_ref[...]
    p = _dot(h, w_ref[...])
    v = p[:, D_MODEL:2 * D_MODEL] * p[:, 2 * D_MODEL:3 * D_MODEL]
    past = CONV_C - 1
    slabs = [st_ref[:, i, :] for i in range(past)] + [v[s * n:(s + 1) * n, :] for s in range(steps)]
    rows = []
    for s in range(steps):
        cc = cw_ref[0:1, :] * slabs[s]
        for j in range(1, CONV_C):
            cc = cc + cw_ref[j:j + 1, :] * slabs[s + j]
        rows.append(cc)
    cc = jnp.concatenate(rows, axis=0)
    for i in range(past):
        nc_ref[:, i, :] = slabs[steps + i]
    y = p[:, :D_MODEL] * cc * _silu(p[:, 3 * D_MODEL:])
    yc_ref[...] = _dot(y.astype(BF16), wo_ref[...])


def _mixer_c_sample(h2, state, w, cw, wo, steps):
    rows, d = h2.shape
    n = rows // steps
    kern = functools.partial(_mixer_c_sample_kernel, steps=steps, n=n)
    return pl.pallas_call(
        kern,
        grid=(1,),
        in_specs=[_full((rows, d)), _full(state.shape), _full(w.shape), _full(cw.shape), _full(wo.shape)],
        out_specs=[_full((rows, d)), _full(state.shape)],
        out_shape=[jax.ShapeDtypeStruct((rows, d), F32), jax.ShapeDtypeStruct(state.shape, F32)],
        compiler_params=_params(("arbitrary",)),
        name="mixer_c_sample",
    )(h2, state, w, cw, wo)


def _merge_kernel(x_ref, h_ref, ya_ref, yb_ref, yc_ref, w_ref, gb_ref, wo_ref, ng_ref, *out_refs, final):
    h = h_ref[0]
    m = None
    for i, y_ref in enumerate((ya_ref, yb_ref, yc_ref)):
        cs = slice(i * D_MODEL, (i + 1) * D_MODEL)
        term = _sigmoid(_dot(h, w_ref[:, cs]) + gb_ref[:, cs]) * y_ref[0]
        m = term if m is None else m + term
    out = x_ref[0] + _dot(m.astype(BF16), wo_ref[...])
    normed = _rms_rows(out, ng_ref[...])
    if final:
        out_refs[0][0] = normed
    else:
        out_refs[0][0] = out
        out_refs[1][0] = normed.astype(BF16)


def _merge(x, h, ya, yb, yc, w, gate_b, wo, next_g, tt, final):
    bsz, seq, d = x.shape
    tile = pl.BlockSpec((1, tt, d), lambda b, t: (b, t, 0))
    kern = functools.partial(_merge_kernel, final=final)
    out_shape = [jax.ShapeDtypeStruct((bsz, seq, d), F32)]
    if not final:
        out_shape.append(jax.ShapeDtypeStruct((bsz, seq, d), BF16))
    outs = pl.pallas_call(
        kern,
        grid=(bsz, seq // tt),
        in_specs=[tile, tile, tile, tile, tile, _full(w.shape), _full(gate_b.shape), _full(wo.shape),
                  _full((1, d))],
        out_specs=[tile] * len(out_shape),
        out_shape=out_shape,
        compiler_params=_params(("arbitrary", "arbitrary")),
        name="merge",
    )(x, h, ya, yb, yc, w, gate_b, wo, next_g)
    return (outs[0], None) if final else (outs[0], outs[1])


def _rmsnorm_kernel(x_ref, g_ref, h_ref):
    h_ref[0] = _rms_rows(x_ref[0], g_ref[...]).astype(BF16)


def _rmsnorm_bf16(x, g, tt):
    bsz, seq, d = x.shape
    tile = pl.BlockSpec((1, tt, d), lambda b, t: (b, t, 0))
    return pl.pallas_call(
        _rmsnorm_kernel,
        grid=(bsz, seq // tt),
        in_specs=[tile, _full((1, d))],
        out_specs=tile,
        out_shape=jax.ShapeDtypeStruct((bsz, seq, d), BF16),
        compiler_params=_params(("arbitrary", "arbitrary")),
        name="rmsnorm_bf16",
    )(x, g)


def _lower_bound(lbraw_ref, layer):
    depth = lbraw_ref.shape[0]
    rows = [lbraw_ref[i:i + 1, :] for i in range(depth)]
    mx = functools.reduce(jnp.maximum, rows)
    ex = [jnp.exp(r - mx) for r in rows]
    tot = functools.reduce(lambda a, b: a + b, ex)
    csum = []
    run = None
    for e in ex:
        sm = e / tot
        run = sm if run is None else run + sm
        csum.append(run)
    return csum[layer] - csum[0]


def _forget_terms(f_pre, lb):
    e = jnp.exp(-jnp.abs(f_pre))
    r = 1.0 / (1.0 + e)
    er = e * r
    pos = f_pre > 0.0
    log_sig = jnp.minimum(f_pre, 0.0) + jnp.log(r)
    one_m_lb = 1.0 - lb
    f = lb + one_m_lb * jnp.where(pos, r, er)
    logf = jnp.where(lb > 0.0, jnp.log(f), log_sig)
    k = one_m_lb * jnp.where(pos, er, r)
    return logf, k


def _split3(x):
    hi = x.astype(BF16)
    r1 = x - hi.astype(F32)
    mid = r1.astype(BF16)
    lo = (r1 - mid.astype(F32)).astype(BF16)
    return hi, mid, lo


def _hgrn_chunk_consts(c):
    levels = c.bit_length() - 1
    t = np.arange(c)
    col = t[None, :]
    row = t[:, None]
    blocks = [col <= row, col > row]
    masks = []
    for j in range(levels):
        half = 1 << j
        p = ((t >> (j + 1)) << (j + 1)) + half
        tgt = ((t >> j) & 1) == 1
        seg_t = (col >= p[:, None]) & (col <= row)
        seg_s = (col > row) & (col <= p[:, None] - 1)
        blocks.append(np.where(tgt[:, None], seg_t, seg_s))
        same = (row >> (j + 1)) == (col >> (j + 1))
        masks.append(same & tgt[:, None] & (~tgt)[None, :])
    assert (np.sum(np.stack(masks), axis=0) == (col < row)).all()
    seg = np.concatenate(blocks, axis=0).astype(np.float32)
    seg3 = np.concatenate([seg, seg, seg], axis=1)
    pair_masks = np.tile(np.stack(masks).astype(np.float32), (1, 1, 2))
    return jnp.asarray(seg3, BF16), jnp.asarray(pair_masks), levels


def _mixer_b_kernel(h_ref, w_ref, lbraw_ref, hgn_ref, wo_ref, seg_ref, mask_ref, *rest,
                    tt, nt, layer, levels):
    yb_ref, nh_ref, q_s, k_s, lf_s, v_s, gz_s, y_s, dec2_s, att2_s, st_s = rest[-11:]
    t = pl.program_id(1)
    c = HG_CHUNK
    w_bk = HG_HEADS * HG_DK

    @pl.when(t == 0)
    def _():
        st_s[...] = jnp.zeros(st_s.shape, F32)

    h = h_ref[0]
    logf, kk = _forget_terms(_dot(h, w_ref[:, w_bk:2 * w_bk]), _lower_bound(lbraw_ref, layer))
    lf_s[...] = logf
    k_s[...] = kk
    gz_s[...] = _silu(_dot(h, w_ref[:, 3 * w_bk:]))
    q_s[...] = _dot(h, w_ref[:, :w_bk])
    v_s[...] = _dot(h, w_ref[:, 2 * w_bk:3 * w_bk]).astype(BF16)

    def chunk(ci, dec_s, att_s):
        rows = pl.ds(pl.multiple_of(ci * c, c), c)
        hi, mid, lo = _split3(lf_s[rows, :])
        expo = _dot(seg_ref[...], jnp.concatenate([hi, mid, lo], axis=0))
        dec_s[...] = jnp.exp(expo)
        zrow = jnp.zeros((c, HG_DK), BF16)
        zblk = jnp.zeros((HG_DV, HG_DK), BF16)

        def blockdiag(m, z):
            return jnp.concatenate([jnp.concatenate([m[:, :HG_DK], z], axis=1),
                                    jnp.concatenate([z, m[:, HG_DK:]], axis=1)], axis=0)

        for pr in range(HG_HEADS // 2):
            cs2 = slice(2 * pr * HG_DK, 2 * (pr + 1) * HG_DK)
            q2 = q_s[rows, cs2]
            k2 = k_s[rows, cs2]
            att = None
            for j in range(levels):
                gj = dec_s[(2 + j) * c:(3 + j) * c, cs2]
                kj = (k2 * gj).astype(BF16)
                pj = lax.dot_general((q2 * gj).astype(BF16), blockdiag(kj, zrow), NT_DIMS,
                                     preferred_element_type=F32) * mask_ref[j]
                att = pj if att is None else att + pj
            att_s[pr] = att.astype(BF16)

        for pr in range(HG_HEADS // 2):
            cs2 = slice(2 * pr * HG_DK, 2 * (pr + 1) * HG_DK)
            q2 = q_s[rows, cs2]
            k2 = k_s[rows, cs2]
            v2 = v_s[rows, cs2]
            qg = (q2 * dec_s[0:c, cs2]).astype(BF16)
            st_t = jnp.concatenate(
                [jnp.concatenate([st_s[2 * pr].astype(BF16), zblk], axis=1),
                 jnp.concatenate([zblk, st_s[2 * pr + 1].astype(BF16)], axis=1)], axis=0)
            o2 = lax.dot_general(qg, st_t, NT_DIMS, preferred_element_type=F32)
            o2 = o2 + _dot(att_s[pr], blockdiag(v2, zrow))
            qk = q2 * k2
            for i in range(2):
                hd = 2 * pr + i
                cs = slice(hd * HG_DK, (hd + 1) * HG_DK)
                hs = slice(i * HG_DK, (i + 1) * HG_DK)
                o = o2[:, hs] + jnp.sum(qk[:, hs], axis=-1, keepdims=True) * v2[:, hs].astype(F32)
                o = o * lax.rsqrt(jnp.mean(o * o, axis=-1, keepdims=True) + EPS) * hgn_ref[:, cs]
                y_s[rows, cs] = (o * gz_s[rows, cs]).astype(BF16)

        for pr in range(HG_HEADS // 2):
            cs2 = slice(2 * pr * HG_DK, 2 * (pr + 1) * HG_DK)
            khat = (k_s[rows, cs2] * dec_s[c:2 * c, cs2]).astype(BF16)
            upd = lax.dot_general(v_s[rows, cs2], khat, TN_DIMS, preferred_element_type=F32)
            last = dec_s[c - 1:c, cs2]
            for i in range(2):
                hs = slice(i * HG_DK, (i + 1) * HG_DK)
                st_s[2 * pr + i] = st_s[2 * pr + i] * last[:, hs] + upd[hs, hs]

    def chunk_group(cg, carry):
        for e in range(HG_SLOTS):
            chunk(HG_SLOTS * cg + e, dec2_s.at[e], att2_s.at[e])
        return carry

    lax.fori_loop(0, tt // (HG_SLOTS * c), chunk_group, 0)
    yb_ref[0] = _dot(y_s[...], wo_ref[...])

    @pl.when(t == nt - 1)
    def _():
        for hd in range(HG_HEADS):
            nh_ref[0, hd] = st_s[hd].T


def _mixer_b_prompt(h, w, lb_raw, hgn, wo, prev, layer, depth, tt):
    bsz, seq, d = h.shape
    nt = seq // tt
    seg3, masks, levels = _hgrn_chunk_consts(HG_CHUNK)
    w_bk = HG_HEADS * HG_DK
    kern = functools.partial(_mixer_b_kernel, tt=tt, nt=nt, layer=layer, levels=levels)
    operands = [h, w, lb_raw, hgn, wo, seg3, masks]
    slot_specs, slot_args, aliases = _stacked_slot(prev, len(operands), 1)
    return pl.pallas_call(
        kern,
        grid=(bsz, nt),
        in_specs=[
            pl.BlockSpec((1, tt, d), lambda b, t: (b, t, 0)),
            _full(w.shape), _full(lb_raw.shape), _full((1, w_bk)), _full(wo.shape),
            _full(seg3.shape), _full(masks.shape),
        ] + slot_specs,
        out_specs=[
            pl.BlockSpec((1, tt, d), lambda b, t: (b, t, 0)),
            pl.BlockSpec((None, 1, HG_HEADS, HG_DK, HG_DV), lambda b, t: (layer, b, 0, 0, 0)),
        ],
        out_shape=[
            jax.ShapeDtypeStruct((bsz, seq, d), F32),
            jax.ShapeDtypeStruct((depth, bsz, HG_HEADS, HG_DK, HG_DV), F32),
        ],
        input_output_aliases=aliases,
        scratch_shapes=[
            pltpu.VMEM((tt, w_bk), F32),
            pltpu.VMEM((tt, w_bk), F32),
            pltpu.VMEM((tt, w_bk), F32),
            pltpu.VMEM((tt, w_bk), BF16),
            pltpu.VMEM((tt, w_bk), F32),
            pltpu.VMEM((tt, w_bk), BF16),
            pltpu.VMEM((HG_SLOTS, (levels + 2) * HG_CHUNK, w_bk), F32),
            pltpu.VMEM((HG_SLOTS, HG_HEADS // 2, HG_CHUNK, 2 * HG_CHUNK), BF16),
            pltpu.VMEM((HG_HEADS, HG_DV, HG_DK), F32),
        ],
        compiler_params=_params(("arbitrary", "arbitrary")),
        name="mixer_b_prompt",
    )(*operands, *slot_args)


def _mixer_b_sample_proj_kernel(h_ref, w_ref, lbraw_ref,
                                qg_ref, kh_ref, v_ref, eb_ref, oi_ref, gz_ref, *, steps, n, layer):
    w_bk = HG_HEADS * HG_DK
    h = h_ref[...]
    p = _dot(h, w_ref[...])
    logf, kk = _forget_terms(p[:, w_bk:2 * w_bk], _lower_bound(lbraw_ref, layer))
    q = p[:, :w_bk]
    v = p[:, 2 * w_bk:3 * w_bk]
    gz_ref[...] = _silu(p[:, 3 * w_bk:])
    v_ref[...] = v.astype(BF16)

    rs = lambda a, s: a[s * n:(s + 1) * n, :]
    bc = []
    for s in range(steps):
        bc.append(rs(logf, s) if s == 0 else bc[-1] + rs(logf, s))
    eb_ref[...] = jnp.exp(bc[-1])
    ones = jnp.ones((HG_DK, HG_DK), BF16)
    for s in range(steps):
        qg_ref[s * n:(s + 1) * n, :] = (rs(q, s) * jnp.exp(bc[s])).astype(BF16)
        kh_ref[s * n:(s + 1) * n, :] = (rs(kk, s) * jnp.exp(bc[-1] - bc[s])).astype(BF16)
        acc = jnp.zeros((n, w_bk), F32)
        for r in range(s + 1):
            pr = rs(q, s) * rs(kk, r)
            if r < s:
                pr = pr * jnp.exp(bc[s] - bc[r])
            pr = pr.astype(BF16)
            att = jnp.concatenate(
                [_dot(pr[:, hd * HG_DK:(hd + 1) * HG_DK], ones) for hd in range(HG_HEADS)], axis=1)
            acc = acc + att * rs(v, r)
        oi_ref[s * n:(s + 1) * n, :] = acc


def _mixer_b_sample_proj(h2, w, lb_raw, layer, steps):
    rows, d = h2.shape
    n = rows // steps
    w_bk = HG_HEADS * HG_DK
    kern = functools.partial(_mixer_b_sample_proj_kernel, steps=steps, n=n, layer=layer)
    return pl.pallas_call(
        kern,
        grid=(1,),
        in_specs=[_full((rows, d)), _full(w.shape), _full(lb_raw.shape)],
        out_specs=[_full((rows, w_bk)), _full((rows, w_bk)), _full((rows, w_bk)), _full((n, w_bk)),
                   _full((rows, w_bk)), _full((rows, w_bk))],
        out_shape=[
            jax.ShapeDtypeStruct((rows, w_bk), BF16),
            jax.ShapeDtypeStruct((rows, w_bk), BF16),
            jax.ShapeDtypeStruct((rows, w_bk), BF16),
            jax.ShapeDtypeStruct((n, w_bk), F32),
            jax.ShapeDtypeStruct((rows, w_bk), F32),
            jax.ShapeDtypeStruct((rows, w_bk), F32),
        ],
        compiler_params=_params(("arbitrary",)),
        name="mixer_b_sample_proj",
    )(h2, w, lb_raw)


def _mixer_b_sample_state_kernel(st_ref, qg_ref, kh_ref, v_ref, eb_ref, *rest, steps, nb):
    io_ref, ns_ref = rest[-2:]
    pad = SUBLANES - 3 - steps
    ones3 = jnp.ones((3, HG_DV), BF16)
    for i in range(nb):
        for hd in range(HG_HEADS):
            cs = slice(hd * HG_DK, (hd + 1) * HG_DK)
            s = st_ref[i, hd]
            io_ref[:, i, cs] = _dot(qg_ref[:, i, cs], s.astype(BF16))
            hi, mid, lo = _split3(eb_ref[i:i + 1, cs])
            lhs = jnp.concatenate([hi, mid, lo, kh_ref[:, i, cs], jnp.zeros((pad, HG_DK), BF16)], axis=0)
            top = jnp.concatenate([ones3, jnp.zeros((3, HG_DV), BF16)], axis=1)
            mid_rows = jnp.concatenate([jnp.zeros((steps, HG_DV), BF16), v_ref[:, i, cs]], axis=1)
            rhs = jnp.concatenate([top, mid_rows, jnp.zeros((pad, 2 * HG_DV), BF16)], axis=0)
            r = lax.dot_general(lhs, rhs, TN_DIMS, preferred_element_type=F32)
            ns_ref[i, hd] = r[:, :HG_DV] * s + r[:, HG_DV:]


def _mixer_b_sample_state(states, qg, kh, v, eb, prev, layer, nb):
    steps, n, w_bk = qg.shape
    kern = functools.partial(_mixer_b_sample_state_kernel, steps=steps, nb=nb)
    tok = pl.BlockSpec((steps, nb, w_bk), lambda i: (0, i, 0))
    st = pl.BlockSpec((None, nb, HG_HEADS, HG_DK, HG_DV), lambda i: (layer, i, 0, 0, 0))
    operands = [states, qg, kh, v, eb]
    slot_specs, slot_args, aliases = _stacked_slot(prev, len(operands), 1)
    return pl.pallas_call(
        kern,
        grid=(n // nb,),
        in_specs=[st, tok, tok, tok, pl.BlockSpec((nb, w_bk), lambda i: (i, 0))] + slot_specs,
        out_specs=[tok, st],
        out_shape=[jax.ShapeDtypeStruct((steps, n, w_bk), F32), jax.ShapeDtypeStruct(states.shape, F32)],
        input_output_aliases=aliases,
        compiler_params=_params(("arbitrary",)),
        name="mixer_b_sample_state",
    )(*operands, *slot_args)


def _mixer_b_sample_out_kernel(io_ref, oi_ref, gz_ref, hgn_ref, wo_ref, yb_ref):
    o = io_ref[...] + oi_ref[...]
    parts = []
    for hd in range(HG_HEADS):
        oh = o[:, hd * HG_DV:(hd + 1) * HG_DV]
        parts.append(oh * lax.rsqrt(jnp.mean(oh * oh, axis=-1, keepdims=True) + EPS))
    y = jnp.concatenate(parts, axis=1) * hgn_ref[...] * gz_ref[...]
    yb_ref[...] = _dot(y.astype(BF16), wo_ref[...])


def _mixer_b_sample_out(inter, intra, gz, hgn, wo):
    rows, w_bv = inter.shape
    return pl.pallas_call(
        _mixer_b_sample_out_kernel,
        grid=(1,),
        in_specs=[_full((rows, w_bv))] * 3 + [_full((1, w_bv)), _full(wo.shape)],
        out_specs=_full((rows, wo.shape[1])),
        out_shape=jax.ShapeDtypeStruct((rows, wo.shape[1]), F32),
        compiler_params=_params(("arbitrary",)),
        name="mixer_b_sample_out",
    )(inter, intra, gz, hgn, wo)


def kernel(x_prompt, x_sample, state_conv_a, state_hgrn, state_conv_c, norm_g, w_in, gate_b, conv_a_w,
           conv_a_b, ln_g, ln_b, w_a_out, lb_raw, hg_norm_g, w_b_out, conv_c_w, w_c_out, w_o, final_norm_g):
    depth = w_in.shape[0]
    d = x_prompt.shape[-1]
    n_seq, steps, _ = x_sample.shape
    w_bk = HG_HEADS * HG_DK
    w_bv = HG_HEADS * HG_DV
    col_a = 3 * d
    col_b = col_a + 2 * w_bk + 2 * w_bv
    col_c = col_b + 4 * d
    row = lambda a: a.reshape(1, -1)
    fin_g = row(final_norm_g)

    rows = steps * n_seq
    xp = x_prompt
    xs = jnp.swapaxes(x_sample, 0, 1).reshape(1, rows, d)
    hp = _rmsnorm_bf16(xp, row(norm_g[0]), tt=xp.shape[1])
    hs = _rmsnorm_bf16(xs, row(norm_g[0]), tt=rows)
    pa, pc, sc = [], [], []
    ph = sa = sh = None
    for l in range(depth):
        cols = lambda a, b: w_in[l, :, a:b].astype(BF16)
        w_a, w_b, w_c, w_g = cols(0, col_a), cols(col_a, col_b), cols(col_b, col_c), cols(col_c, None)
        wao, wbo, wco, woo = (w_a_out[l].astype(BF16), w_b_out[l].astype(BF16), w_c_out[l].astype(BF16),
                              w_o[l].astype(BF16))
        gb = row(gate_b[l])
        cab, lng, lnb, hgn = row(conv_a_b[l]), row(ln_g[l]), row(ln_b[l]), row(hg_norm_g[l])
        final = l == depth - 1
        next_g = fin_g if final else row(norm_g[l + 1])

        ya, na = _mixer_a_prompt(hp, w_a, conv_a_w[l], cab, lng, lnb, wao, tt=PROMPT_TILE)
        yb, ph = _mixer_b_prompt(hp, w_b, lb_raw, hgn, wbo, ph, layer=l, depth=depth, tt=PROMPT_TILE)
        yc, nc = _mixer_c_prompt(hp, w_c, conv_c_w[l], wco, tt=PROMPT_TILE)
        xp, hp = _merge(xp, hp, ya, yb, yc, w_g, gb, woo, next_g, tt=PROMPT_TILE, final=final)
        pa.append(na); pc.append(nc)

        hs2 = hs.reshape(rows, d)
        ya, sa = _mixer_a_sample(hs.reshape(steps, n_seq, d), state_conv_a, w_a, conv_a_w[l], cab, lng, lnb,
                                 wao, sa, layer=l, nb=SAMPLE_CONV_SEQS)
        qg, kh, v, eb, intra, gz = _mixer_b_sample_proj(hs2, w_b, lb_raw, layer=l, steps=steps)
        tm = lambda a: a.reshape(steps, n_seq, w_bk)
        inter, sh = _mixer_b_sample_state(state_hgrn, tm(qg), tm(kh), tm(v), eb, sh, layer=l,
                                          nb=SAMPLE_STATE_SEQS)
        yb = _mixer_b_sample_out(inter.reshape(rows, w_bv), intra, gz, hgn, wbo)
        yc, nc = _mixer_c_sample(hs2, state_conv_c[l], w_c, conv_c_w[l], wco, steps=steps)
        tile = lambda a: a.reshape(1, rows, d)
        xs, hs = _merge(xs, hs, tile(ya), tile(yb), tile(yc), w_g, gb, woo, next_g, tt=rows, final=final)
        sc.append(nc)

    y_sample = jnp.swapaxes(xs.reshape(steps, n_seq, d), 0, 1)
    return (xp, y_sample, jnp.stack(pa), ph, jnp.stack(pc), sa, sh, jnp.stack(sc))
```

```python
import functools
from typing import NamedTuple

import numpy as np
import jax
import jax.numpy as jnp
from jax import lax
from jax.experimental import pallas as pl
from jax.experimental.pallas import tpu as pltpu

F32 = jnp.float32
BF16 = jnp.bfloat16

EPS = 1e-6
D_MODEL = 1024
HG_HEADS = 8
HG_DK = 128
HG_DV = 128
CONV_A = 31
CONV_C = 3

SUBLANES = 8
LANES = 128
HALO_A = 32
HALO_C = 8
CONV_ROWS = 64
HG_CHUNK = 64
HG_SLOTS = 4
PROMPT_TILE = 512
SAMPLE_CONV_SEQS = 32
SAMPLE_STATE_SEQS = 8
VMEM_LIMIT = 56 * 1024 * 1024

NT_DIMS = (((1,), (1,)), ((), ()))
TN_DIMS = (((0,), (0,)), ((), ()))


def _sigmoid(x):
    return 1.0 / (1.0 + jnp.exp(-x))


def _silu(x):
    return x * _sigmoid(x)


def _rms_rows(x, g):
    ms = jnp.mean(x * x, axis=-1, keepdims=True)
    return x * lax.rsqrt(ms + EPS) * g


def _layernorm_rows(x, g, b):
    mu = jnp.mean(x, axis=-1, keepdims=True)
    xc = x - mu
    var = jnp.mean(xc * xc, axis=-1, keepdims=True)
    return xc * lax.rsqrt(var + EPS) * g + b


def _dot(a, b):
    return jnp.dot(a, b, preferred_element_type=F32)


def _params(sem):
    return pltpu.CompilerParams(dimension_semantics=sem, vmem_limit_bytes=VMEM_LIMIT,
                                )


def _full(shape):
    n = len(shape)
    return pl.BlockSpec(shape, lambda *_: (0,) * n)


class _Cols(NamedTuple):
    array: jax.Array
    row0: int
    rows: int
    col0: int
    width: int

    def spec(self):
        return pl.BlockSpec((pl.Element(self.rows), pl.Element(self.width)), lambda *_: (self.row0, self.col0))


def _stacked_slot(prev, n_inputs, out_index):
    if prev is None:
        return [], [], {}
    return [pl.BlockSpec(memory_space=pl.ANY)], [prev], {n_inputs: out_index}


def _mixer_a_kernel(h_ref, w_ref, cw_ref, cb_ref, lng_ref, lnb_ref, wo_ref,
                    ya_ref, na_ref, ubuf, cabuf, shbuf, *, tt, nt):
    t = pl.program_id(1)

    @pl.when(t == 0)
    def _():
        ubuf[0:HALO_A, :] = jnp.zeros((HALO_A, D_MODEL), F32)

    h = h_ref[0]
    gate = _sigmoid(_dot(h, w_ref[:, D_MODEL:2 * D_MODEL]))
    ubuf[HALO_A:HALO_A + tt, :] = _dot(h, w_ref[:, :D_MODEL]) * gate
    gz = _silu(_dot(h, w_ref[:, 2 * D_MODEL:]))

    off = HALO_A - (CONV_A - 1)

    nsh = HALO_A + tt - SUBLANES

    def lane_tile(c, carry):
        cs = pl.ds(pl.multiple_of(c * LANES, LANES), LANES)
        for r in range(1, SUBLANES):
            shbuf[r - 1, :, cs] = ubuf[r:r + nsh, cs]
        for rb in range(tt // CONV_ROWS):
            r0 = rb * CONV_ROWS
            acc = jnp.broadcast_to(cb_ref[:, cs], (CONV_ROWS, LANES))
            for j in range(CONV_A):
                base, r = divmod(off + j, SUBLANES)
                lo = r0 + base * SUBLANES
                win = ubuf[lo:lo + CONV_ROWS, cs] if r == 0 else shbuf[r - 1, lo:lo + CONV_ROWS, cs]
                acc = acc + cw_ref[j:j + 1, cs] * win
            cabuf[r0:r0 + CONV_ROWS, cs] = acc
        return carry

    lax.fori_loop(0, D_MODEL // LANES, lane_tile, 0)

    @pl.when(t == nt - 1)
    def _():
        na_ref[0] = ubuf[tt + off:tt + HALO_A, :]

    ubuf[0:HALO_A, :] = ubuf[tt:tt + HALO_A, :]

    y = _silu(_layernorm_rows(cabuf[...], lng_ref[...], lnb_ref[...])) * gz
    ya_ref[0] = _dot(y.astype(BF16), wo_ref[...])


def _mixer_a_prompt(h, w, cw, cb, lng, lnb, wo, tt):
    bsz, seq, d = h.shape
    nt = seq // tt
    kern = functools.partial(_mixer_a_kernel, tt=tt, nt=nt)
    return pl.pallas_call(
        kern,
        grid=(bsz, nt),
        in_specs=[
            pl.BlockSpec((1, tt, d), lambda b, t: (b, t, 0)),
            w.spec(), _full(cw.shape), _full((1, d)), _full((1, d)), _full((1, d)),
            _full(wo.shape),
        ],
        out_specs=[
            pl.BlockSpec((1, tt, d), lambda b, t: (b, t, 0)),
            pl.BlockSpec((1, CONV_A - 1, d), lambda b, t: (b, 0, 0)),
        ],
        out_shape=[
            jax.ShapeDtypeStruct((bsz, seq, d), F32),
            jax.ShapeDtypeStruct((bsz, CONV_A - 1, d), F32),
        ],
        scratch_shapes=[
            pltpu.VMEM((HALO_A + tt, d), F32),
            pltpu.VMEM((tt, d), F32),
            pltpu.VMEM((SUBLANES - 1, HALO_A + tt - SUBLANES, d), F32),
        ],
        compiler_params=_params(("arbitrary", "arbitrary")),
        name="mixer_a_prompt",
    )(h, w.array, cw, cb, lng, lnb, wo)


def _mixer_a_sample_kernel(h_ref, st_ref, w_ref, cw_ref, cb_ref, lng_ref, lnb_ref, wo_ref, *rest,
                           steps, nb):
    ya_ref, na_ref = rest[-2:]
    h = h_ref[...].reshape(steps * nb, D_MODEL)
    p = _dot(h, w_ref[...])
    u = p[:, :D_MODEL] * _sigmoid(p[:, D_MODEL:2 * D_MODEL])
    gz = _silu(p[:, 2 * D_MODEL:])
    past = CONV_A - 1

    acc = [jnp.broadcast_to(cb_ref[...], (nb, D_MODEL)) for _ in range(steps)]
    for i in range(past + steps):
        slab = st_ref[i] if i < past else u[(i - past) * nb:(i - past + 1) * nb, :]
        for s in range(steps):
            j = i - s
            if 0 <= j < CONV_A:
                acc[s] = acc[s] + cw_ref[j:j + 1, :] * slab
    ca = jnp.concatenate(acc, axis=0)

    na_ref[0:past - steps] = st_ref[steps:past]
    for s in range(steps):
        na_ref[past - steps + s] = u[s * nb:(s + 1) * nb, :]

    y = _silu(_layernorm_rows(ca, lng_ref[...], lnb_ref[...])) * gz
    ya_ref[...] = _dot(y.astype(BF16), wo_ref[...]).reshape(steps, nb, D_MODEL)


def _mixer_a_sample(h_tm, states, w, cw, cb, lng, lnb, wo, prev, layer, nb):
    steps, n, d = h_tm.shape
    kern = functools.partial(_mixer_a_sample_kernel, steps=steps, nb=nb)
    st = pl.BlockSpec((None, CONV_A - 1, nb, d), lambda i: (layer, 0, i, 0))
    operands = [h_tm, states, w.array, cw, cb, lng, lnb, wo]
    slot_specs, slot_args, aliases = _stacked_slot(prev, len(operands), 1)
    return pl.pallas_call(
        kern,
        grid=(n // nb,),
        in_specs=[
            pl.BlockSpec((steps, nb, d), lambda i: (0, i, 0)),
            st,
            w.spec(), _full(cw.shape), _full((1, d)), _full((1, d)), _full((1, d)),
            _full(wo.shape),
        ] + slot_specs,
        out_specs=[pl.BlockSpec((steps, nb, d), lambda i: (0, i, 0)), st],
        out_shape=[jax.ShapeDtypeStruct((steps, n, d), F32), jax.ShapeDtypeStruct(states.shape, F32)],
        input_output_aliases=aliases,
        compiler_params=_params(("arbitrary",)),
        name="mixer_a_sample",
    )(*operands, *slot_args)


def _mixer_c_kernel(h_ref, w_ref, cw_ref, wo_ref, yc_ref, nc_ref, vbuf, *, tt, nt):
    t = pl.program_id(1)

    @pl.when(t == 0)
    def _():
        vbuf[0:HALO_C, :] = jnp.zeros((HALO_C, D_MODEL), F32)

    h = h_ref[0]
    vbuf[HALO_C:HALO_C + tt, :] = (_dot(h, w_ref[:, D_MODEL:2 * D_MODEL])
                                   * _dot(h, w_ref[:, 2 * D_MODEL:3 * D_MODEL]))
    off = HALO_C - (CONV_C - 1)
    cc = cw_ref[0:1, :] * vbuf[off:off + tt, :]
    for j in range(1, CONV_C):
        cc = cc + cw_ref[j:j + 1, :] * vbuf[off + j:off + j + tt, :]

    @pl.when(t == nt - 1)
    def _():
        nc_ref[0] = vbuf[tt + off:tt + HALO_C, :]

    vbuf[0:HALO_C, :] = vbuf[tt:tt + HALO_C, :]
    y = _dot(h, w_ref[:, :D_MODEL]) * cc * _silu(_dot(h, w_ref[:, 3 * D_MODEL:]))
    yc_ref[0] = _dot(y.astype(BF16), wo_ref[...])


def _mixer_c_prompt(h, w, cw, wo, tt):
    bsz, seq, d = h.shape
    nt = seq // tt
    kern = functools.partial(_mixer_c_kernel, tt=tt, nt=nt)
    return pl.pallas_call(
        kern,
        grid=(bsz, nt),
        in_specs=[
            pl.BlockSpec((1, tt, d), lambda b, t: (b, t, 0)),
            w.spec(), _full(cw.shape), _full(wo.shape),
        ],
        out_specs=[
            pl.BlockSpec((1, tt, d), lambda b, t: (b, t, 0)),
            pl.BlockSpec((1, CONV_C - 1, d), lambda b, t: (b, 0, 0)),
        ],
        out_shape=[
            jax.ShapeDtypeStruct((bsz, seq, d), F32),
            jax.ShapeDtypeStruct((bsz, CONV_C - 1, d), F32),
        ],
        scratch_shapes=[pltpu.VMEM((HALO_C + tt, d), F32)],
        compiler_params=_params(("arbitrary", "arbitrary")),
        name="mixer_c_prompt",
    )(h, w.array, cw, wo)


def _mixer_c_sample_kernel(h_ref, st_ref, w_ref, cw_ref, wo_ref, yc_ref, nc_ref, *, steps, n):
    h = h_ref[...]
    p = _dot(h, w_ref[...])
    v = p[:, D_MODEL:2 * D_MODEL] * p[:, 2 * D_MODEL:3 * D_MODEL]
    past = CONV_C - 1
    slabs = [st_ref[:, i, :] for i in range(past)] + [v[s * n:(s + 1) * n, :] for s in range(steps)]
    rows = []
    for s in range(steps):
        cc = cw_ref[0:1, :] * slabs[s]
        for j in range(1, CONV_C):
            cc = cc + cw_ref[j:j + 1, :] * slabs[s + j]
        rows.append(cc)
    cc = jnp.concatenate(rows, axis=0)
    for i in range(past):
        nc_ref[:, i, :] = slabs[steps + i]
    y = p[:, :D_MODEL] * cc * _silu(p[:, 3 * D_MODEL:])
    yc_ref[...] = _dot(y.astype(BF16), wo_ref[...])


def _mixer_c_sample(h2, state, w, cw, wo, steps):
    rows, d = h2.shape
    n = rows // steps
    kern = functools.partial(_mixer_c_sample_kernel, steps=steps, n=n)
    return pl.pallas_call(
        kern,
        grid=(1,),
        in_specs=[_full((rows, d)), _full(state.shape), w.spec(), _full(cw.shape), _full(wo.shape)],
        out_specs=[_full((rows, d)), _full(state.shape)],
        out_shape=[jax.ShapeDtypeStruct((rows, d), F32), jax.ShapeDtypeStruct(state.shape, F32)],
        compiler_params=_params(("arbitrary",)),
        name="mixer_c_sample",
    )(h2, state, w.array, cw, wo)


def _merge_kernel(x_ref, h_ref, ya_ref, yb_ref, yc_ref, w_ref, gb_ref, wo_ref, ng_ref, *out_refs, final):
    h = h_ref[0]
    m = None
    for i, y_ref in enumerate((ya_ref, yb_ref, yc_ref)):
        cs = slice(i * D_MODEL, (i + 1) * D_MODEL)
        term = _sigmoid(_dot(h, w_ref[:, cs]) + gb_ref[:, cs]) * y_ref[0]
        m = term if m is None else m + term
    out = x_ref[0] + _dot(m.astype(BF16), wo_ref[...])
    normed = _rms_rows(out, ng_ref[...])
    if final:
        out_refs[0][0] = normed
    else:
        out_refs[0][0] = out
        out_refs[1][0] = normed.astype(BF16)


def _merge(x, h, ya, yb, yc, w, gate_b, wo, next_g, tt, final):
    bsz, seq, d = x.shape
    tile = pl.BlockSpec((1, tt, d), lambda b, t: (b, t, 0))
    kern = functools.partial(_merge_kernel, final=final)
    out_shape = [jax.ShapeDtypeStruct((bsz, seq, d), F32)]
    if not final:
        out_shape.append(jax.ShapeDtypeStruct((bsz, seq, d), BF16))
    outs = pl.pallas_call(
        kern,
        grid=(bsz, seq // tt),
        in_specs=[tile, tile, tile, tile, tile, w.spec(), _full(gate_b.shape), _full(wo.shape),
                  _full((1, d))],
        out_specs=[tile] * len(out_shape),
        out_shape=out_shape,
        compiler_params=_params(("arbitrary", "arbitrary")),
        name="merge",
    )(x, h, ya, yb, yc, w.array, gate_b, wo, next_g)
    return (outs[0], None) if final else (outs[0], outs[1])


def _rmsnorm_kernel(x_ref, g_ref, h_ref):
    h_ref[0] = _rms_rows(x_ref[0], g_ref[...]).astype(BF16)


def _rmsnorm_bf16(x, g, tt):
    bsz, seq, d = x.shape
    tile = pl.BlockSpec((1, tt, d), lambda b, t: (b, t, 0))
    return pl.pallas_call(
        _rmsnorm_kernel,
        grid=(bsz, seq // tt),
        in_specs=[tile, _full((1, d))],
        out_specs=tile,
        out_shape=jax.ShapeDtypeStruct((bsz, seq, d), BF16),
        compiler_params=_params(("arbitrary", "arbitrary")),
        name="rmsnorm_bf16",
    )(x, g)


def _lower_bound(lbraw_ref, layer):
    depth = lbraw_ref.shape[0]
    rows = [lbraw_ref[i:i + 1, :] for i in range(depth)]
    mx = functools.reduce(jnp.maximum, rows)
    ex = [jnp.exp(r - mx) for r in rows]
    tot = functools.reduce(lambda a, b: a + b, ex)
    csum = []
    run = None
    for e in ex:
        sm = e / tot
        run = sm if run is None else run + sm
        csum.append(run)
    return csum[layer] - csum[0]


def _forget_terms(f_pre, lb):
    e = jnp.exp(-jnp.abs(f_pre))
    r = 1.0 / (1.0 + e)
    er = e * r
    pos = f_pre > 0.0
    log_sig = jnp.minimum(f_pre, 0.0) + jnp.log(r)
    one_m_lb = 1.0 - lb
    f = lb + one_m_lb * jnp.where(pos, r, er)
    logf = jnp.where(lb > 0.0, jnp.log(f), log_sig)
    k = one_m_lb * jnp.where(pos, er, r)
    return logf, k


def _split3(x):
    hi = x.astype(BF16)
    r1 = x - hi.astype(F32)
    mid = r1.astype(BF16)
    lo = (r1 - mid.astype(F32)).astype(BF16)
    return hi, mid, lo


def _hgrn_chunk_consts(c):
    levels = c.bit_length() - 1
    t = np.arange(c)
    col = t[None, :]
    row = t[:, None]
    blocks = [col <= row, col > row]
    masks = []
    for j in range(levels):
        half = 1 << j
        p = ((t >> (j + 1)) << (j + 1)) + half
        tgt = ((t >> j) & 1) == 1
        seg_t = (col >= p[:, None]) & (col <= row)
        seg_s = (col > row) & (col <= p[:, None] - 1)
        blocks.append(np.where(tgt[:, None], seg_t, seg_s))
        same = (row >> (j + 1)) == (col >> (j + 1))
        masks.append(same & tgt[:, None] & (~tgt)[None, :])
    assert (np.sum(np.stack(masks), axis=0) == (col < row)).all()
    seg = np.concatenate(blocks, axis=0).astype(np.float32)
    seg3 = np.concatenate([seg, seg, seg], axis=1)
    pair_masks = np.tile(np.stack(masks).astype(np.float32), (1, 1, 2))
    return jnp.asarray(seg3, BF16), jnp.asarray(pair_masks), levels


def _mixer_b_kernel(h_ref, w_ref, lbraw_ref, hgn_ref, wo_ref, seg_ref, mask_ref, *rest,
                    tt, nt, layer, levels):
    yb_ref, nh_ref, q_s, k_s, lf_s, v_s, gz_s, y_s, dec2_s, att2_s, st_s = rest[-11:]
    t = pl.program_id(1)
    c = HG_CHUNK
    w_bk = HG_HEADS * HG_DK

    @pl.when(t == 0)
    def _():
        st_s[...] = jnp.zeros(st_s.shape, F32)

    h = h_ref[0]
    logf, kk = _forget_terms(_dot(h, w_ref[:, w_bk:2 * w_bk]), _lower_bound(lbraw_ref, layer))
    lf_s[...] = logf
    k_s[...] = kk
    gz_s[...] = _silu(_dot(h, w_ref[:, 3 * w_bk:]))
    q_s[...] = _dot(h, w_ref[:, :w_bk])
    v_s[...] = _dot(h, w_ref[:, 2 * w_bk:3 * w_bk]).astype(BF16)

    def chunk(ci, dec_s, att_s):
        rows = pl.ds(pl.multiple_of(ci * c, c), c)
        hi, mid, lo = _split3(lf_s[rows, :])
        expo = _dot(seg_ref[...], jnp.concatenate([hi, mid, lo], axis=0))
        dec_s[...] = jnp.exp(expo)
        zrow = jnp.zeros((c, HG_DK), BF16)
        zblk = jnp.zeros((HG_DV, HG_DK), BF16)

        def blockdiag(m, z):
            return jnp.concatenate([jnp.concatenate([m[:, :HG_DK], z], axis=1),
                                    jnp.concatenate([z, m[:, HG_DK:]], axis=1)], axis=0)

        for pr in range(HG_HEADS // 2):
            cs2 = slice(2 * pr * HG_DK, 2 * (pr + 1) * HG_DK)
            q2 = q_s[rows, cs2]
            k2 = k_s[rows, cs2]
            att = None
            for j in range(levels):
                gj = dec_s[(2 + j) * c:(3 + j) * c, cs2]
                kj = (k2 * gj).astype(BF16)
                pj = lax.dot_general((q2 * gj).astype(BF16), blockdiag(kj, zrow), NT_DIMS,
                                     preferred_element_type=F32) * mask_ref[j]
                att = pj if att is None else att + pj
            att_s[pr] = att.astype(BF16)

        for pr in range(HG_HEADS // 2):
            cs2 = slice(2 * pr * HG_DK, 2 * (pr + 1) * HG_DK)
            q2 = q_s[rows, cs2]
            k2 = k_s[rows, cs2]
            v2 = v_s[rows, cs2]
            qg = (q2 * dec_s[0:c, cs2]).astype(BF16)
            st_t = jnp.concatenate(
                [jnp.concatenate([st_s[2 * pr].astype(BF16), zblk], axis=1),
                 jnp.concatenate([zblk, st_s[2 * pr + 1].astype(BF16)], axis=1)], axis=0)
            o2 = lax.dot_general(qg, st_t, NT_DIMS, preferred_element_type=F32)
            o2 = o2 + _dot(att_s[pr], blockdiag(v2, zrow))
            qk = q2 * k2
            for i in range(2):
                hd = 2 * pr + i
                cs = slice(hd * HG_DK, (hd + 1) * HG_DK)
                hs = slice(i * HG_DK, (i + 1) * HG_DK)
                o = o2[:, hs] + jnp.sum(qk[:, hs], axis=-1, keepdims=True) * v2[:, hs].astype(F32)
                o = o * lax.rsqrt(jnp.mean(o * o, axis=-1, keepdims=True) + EPS) * hgn_ref[:, cs]
                y_s[rows, cs] = (o * gz_s[rows, cs]).astype(BF16)

        for pr in range(HG_HEADS // 2):
            cs2 = slice(2 * pr * HG_DK, 2 * (pr + 1) * HG_DK)
            khat = (k_s[rows, cs2] * dec_s[c:2 * c, cs2]).astype(BF16)
            upd = lax.dot_general(v_s[rows, cs2], khat, TN_DIMS, preferred_element_type=F32)
            last = dec_s[c - 1:c, cs2]
            for i in range(2):
                hs = slice(i * HG_DK, (i + 1) * HG_DK)
                st_s[2 * pr + i] = st_s[2 * pr + i] * last[:, hs] + upd[hs, hs]

    def chunk_group(cg, carry):
        for e in range(HG_SLOTS):
            chunk(HG_SLOTS * cg + e, dec2_s.at[e], att2_s.at[e])
        return carry

    lax.fori_loop(0, tt // (HG_SLOTS * c), chunk_group, 0)
    yb_ref[0] = _dot(y_s[...], wo_ref[...])

    @pl.when(t == nt - 1)
    def _():
        for hd in range(HG_HEADS):
            nh_ref[0, hd] = st_s[hd].T


def _mixer_b_prompt(h, w, lb_raw, hgn, wo, prev, layer, depth, tt):
    bsz, seq, d = h.shape
    nt = seq // tt
    seg3, masks, levels = _hgrn_chunk_consts(HG_CHUNK)
    w_bk = HG_HEADS * HG_DK
    kern = functools.partial(_mixer_b_kernel, tt=tt, nt=nt, layer=layer, levels=levels)
    operands = [h, w.array, lb_raw, hgn, wo, seg3, masks]
    slot_specs, slot_args, aliases = _stacked_slot(prev, len(operands), 1)
    return pl.pallas_call(
        kern,
        grid=(bsz, nt),
        in_specs=[
            pl.BlockSpec((1, tt, d), lambda b, t: (b, t, 0)),
            w.spec(), _full(lb_raw.shape), _full((1, w_bk)), _full(wo.shape),
            _full(seg3.shape), _full(masks.shape),
        ] + slot_specs,
        out_specs=[
            pl.BlockSpec((1, tt, d), lambda b, t: (b, t, 0)),
            pl.BlockSpec((None, 1, HG_HEADS, HG_DK, HG_DV), lambda b, t: (layer, b, 0, 0, 0)),
        ],
        out_shape=[
            jax.ShapeDtypeStruct((bsz, seq, d), F32),
            jax.ShapeDtypeStruct((depth, bsz, HG_HEADS, HG_DK, HG_DV), F32),
        ],
        input_output_aliases=aliases,
        scratch_shapes=[
            pltpu.VMEM((tt, w_bk), F32),
            pltpu.VMEM((tt, w_bk), F32),
            pltpu.VMEM((tt, w_bk), F32),
            pltpu.VMEM((tt, w_bk), BF16),
            pltpu.VMEM((tt, w_bk), F32),
            pltpu.VMEM((tt, w_bk), BF16),
            pltpu.VMEM((HG_SLOTS, (levels + 2) * HG_CHUNK, w_bk), F32),
            pltpu.VMEM((HG_SLOTS, HG_HEADS // 2, HG_CHUNK, 2 * HG_CHUNK), BF16),
            pltpu.VMEM((HG_HEADS, HG_DV, HG_DK), F32),
        ],
        compiler_params=_params(("arbitrary", "arbitrary")),
        name="mixer_b_prompt",
    )(*operands, *slot_args)


def _mixer_b_sample_proj_kernel(h_ref, w_ref, lbraw_ref,
                                qg_ref, kh_ref, v_ref, eb_ref, oi_ref, gz_ref, *, steps, n, layer):
    w_bk = HG_HEADS * HG_DK
    h = h_ref[...]
    p = _dot(h, w_ref[...])
    logf, kk = _forget_terms(p[:, w_bk:2 * w_bk], _lower_bound(lbraw_ref, layer))
    q = p[:, :w_bk]
    v = p[:, 2 * w_bk:3 * w_bk]
    gz_ref[...] = _silu(p[:, 3 * w_bk:])
    v_ref[...] = v.astype(BF16)

    rs = lambda a, s: a[s * n:(s + 1) * n, :]
    bc = []
    for s in range(steps):
        bc.append(rs(logf, s) if s == 0 else bc[-1] + rs(logf, s))
    eb_ref[...] = jnp.exp(bc[-1])
    ones = jnp.ones((HG_DK, HG_DK), BF16)
    for s in range(steps):
        qg_ref[s * n:(s + 1) * n, :] = (rs(q, s) * jnp.exp(bc[s])).astype(BF16)
        kh_ref[s * n:(s + 1) * n, :] = (rs(kk, s) * jnp.exp(bc[-1] - bc[s])).astype(BF16)
        acc = jnp.zeros((n, w_bk), F32)
        for r in range(s + 1):
            pr = rs(q, s) * rs(kk, r)
            if r < s:
                pr = pr * jnp.exp(bc[s] - bc[r])
            pr = pr.astype(BF16)
            att = jnp.concatenate(
                [_dot(pr[:, hd * HG_DK:(hd + 1) * HG_DK], ones) for hd in range(HG_HEADS)], axis=1)
            acc = acc + att * rs(v, r)
        oi_ref[s * n:(s + 1) * n, :] = acc


def _mixer_b_sample_proj(h2, w, lb_raw, layer, steps):
    rows, d = h2.shape
    n = rows // steps
    w_bk = HG_HEADS * HG_DK
    kern = functools.partial(_mixer_b_sample_proj_kernel, steps=steps, n=n, layer=layer)
    return pl.pallas_call(
        kern,
        grid=(1,),
        in_specs=[_full((rows, d)), w.spec(), _full(lb_raw.shape)],
        out_specs=[_full((rows, w_bk)), _full((rows, w_bk)), _full((rows, w_bk)), _full((n, w_bk)),
                   _full((rows, w_bk)), _full((rows, w_bk))],
        out_shape=[
            jax.ShapeDtypeStruct((rows, w_bk), BF16),
            jax.ShapeDtypeStruct((rows, w_bk), BF16),
            jax.ShapeDtypeStruct((rows, w_bk), BF16),
            jax.ShapeDtypeStruct((n, w_bk), F32),
            jax.ShapeDtypeStruct((rows, w_bk), F32),
            jax.ShapeDtypeStruct((rows, w_bk), F32),
        ],
        compiler_params=_params(("arbitrary",)),
        name="mixer_b_sample_proj",
    )(h2, w.array, lb_raw)


def _mixer_b_sample_state_kernel(st_ref, qg_ref, kh_ref, v_ref, eb_ref, *rest, steps, nb):
    io_ref, ns_ref = rest[-2:]
    pad = SUBLANES - 3 - steps
    ones3 = jnp.ones((3, HG_DV), BF16)
    for i in range(nb):
        for hd in range(HG_HEADS):
            cs = slice(hd * HG_DK, (hd + 1) * HG_DK)
            s = st_ref[i, hd]
            io_ref[:, i, cs] = _dot(qg_ref[:, i, cs], s.astype(BF16))
            hi, mid, lo = _split3(eb_ref[i:i + 1, cs])
            lhs = jnp.concatenate([hi, mid, lo, kh_ref[:, i, cs], jnp.zeros((pad, HG_DK), BF16)], axis=0)
            top = jnp.concatenate([ones3, jnp.zeros((3, HG_DV), BF16)], axis=1)
            mid_rows = jnp.concatenate([jnp.zeros((steps, HG_DV), BF16), v_ref[:, i, cs]], axis=1)
            rhs = jnp.concatenate([top, mid_rows, jnp.zeros((pad, 2 * HG_DV), BF16)], axis=0)
            r = lax.dot_general(lhs, rhs, TN_DIMS, preferred_element_type=F32)
            ns_ref[i, hd] = r[:, :HG_DV] * s + r[:, HG_DV:]


def _mixer_b_sample_state(states, qg, kh, v, eb, prev, layer, nb):
    steps, n, w_bk = qg.shape
    kern = functools.partial(_mixer_b_sample_state_kernel, steps=steps, nb=nb)
    tok = pl.BlockSpec((steps, nb, w_bk), lambda i: (0, i, 0))
    st = pl.BlockSpec((None, nb, HG_HEADS, HG_DK, HG_DV), lambda i: (layer, i, 0, 0, 0))
    operands = [states, qg, kh, v, eb]
    slot_specs, slot_args, aliases = _stacked_slot(prev, len(operands), 1)
    return pl.pallas_call(
        kern,
        grid=(n // nb,),
        in_specs=[st, tok, tok, tok, pl.BlockSpec((nb, w_bk), lambda i: (i, 0))] + slot_specs,
        out_specs=[tok, st],
        out_shape=[jax.ShapeDtypeStruct((steps, n, w_bk), F32), jax.ShapeDtypeStruct(states.shape, F32)],
        input_output_aliases=aliases,
        compiler_params=_params(("arbitrary",)),
        name="mixer_b_sample_state",
    )(*operands, *slot_args)


def _mixer_b_sample_out_kernel(io_ref, oi_ref, gz_ref, hgn_ref, wo_ref, yb_ref):
    o = io_ref[...] + oi_ref[...]
    parts = []
    for hd in range(HG_HEADS):
        oh = o[:, hd * HG_DV:(hd + 1) * HG_DV]
        parts.append(oh * lax.rsqrt(jnp.mean(oh * oh, axis=-1, keepdims=True) + EPS))
    y = jnp.concatenate(parts, axis=1) * hgn_ref[...] * gz_ref[...]
    yb_ref[...] = _dot(y.astype(BF16), wo_ref[...])


def _mixer_b_sample_out(inter, intra, gz, hgn, wo):
    rows, w_bv = inter.shape
    return pl.pallas_call(
        _mixer_b_sample_out_kernel,
        grid=(1,),
        in_specs=[_full((rows, w_bv))] * 3 + [_full((1, w_bv)), _full(wo.shape)],
        out_specs=_full((rows, wo.shape[1])),
        out_shape=jax.ShapeDtypeStruct((rows, wo.shape[1]), F32),
        compiler_params=_params(("arbitrary",)),
        name="mixer_b_sample_out",
    )(inter, intra, gz, hgn, wo)


def kernel(x_prompt, x_sample, state_conv_a, state_hgrn, state_conv_c, norm_g, w_in, gate_b, conv_a_w,
           conv_a_b, ln_g, ln_b, w_a_out, lb_raw, hg_norm_g, w_b_out, conv_c_w, w_c_out, w_o, final_norm_g):
    depth = w_in.shape[0]
    d = x_prompt.shape[-1]
    n_seq, steps, _ = x_sample.shape
    w_bk = HG_HEADS * HG_DK
    w_bv = HG_HEADS * HG_DV
    col_a = 3 * d
    col_b = col_a + 2 * w_bk + 2 * w_bv
    col_c = col_b + 4 * d
    row = lambda a: a.reshape(1, -1)
    fin_g = row(final_norm_g)

    rows = steps * n_seq
    xp = x_prompt
    xs = jnp.swapaxes(x_sample, 0, 1).reshape(1, rows, d)
    hp = _rmsnorm_bf16(xp, row(norm_g[0]), tt=xp.shape[1])
    hs = _rmsnorm_bf16(xs, row(norm_g[0]), tt=rows)
    conv_a_tm = jnp.swapaxes(state_conv_a, 1, 2)
    pa, pc, sc = [], [], []
    ph = sa = sh = None
    w_in_bf = w_in.astype(BF16).reshape(depth * d, w_in.shape[-1])
    for l in range(depth):
        cols = lambda a, b: _Cols(w_in_bf, l * d, d, a, b - a)
        w_a, w_b, w_c, w_g = (cols(0, col_a), cols(col_a, col_b), cols(col_b, col_c),
                              cols(col_c, w_in.shape[-1]))
        wao, wbo, wco, woo = (w_a_out[l].astype(BF16), w_b_out[l].astype(BF16), w_c_out[l].astype(BF16),
                              w_o[l].astype(BF16))
        gb = row(gate_b[l])
        cab, lng, lnb, hgn = row(conv_a_b[l]), row(ln_g[l]), row(ln_b[l]), row(hg_norm_g[l])
        final = l == depth - 1
        next_g = fin_g if final else row(norm_g[l + 1])

        ya, na = _mixer_a_prompt(hp, w_a, conv_a_w[l], cab, lng, lnb, wao, tt=PROMPT_TILE)
        yb, ph = _mixer_b_prompt(hp, w_b, lb_raw, hgn, wbo, ph, layer=l, depth=depth, tt=PROMPT_TILE)
        yc, nc = _mixer_c_prompt(hp, w_c, conv_c_w[l], wco, tt=PROMPT_TILE)
        xp, hp = _merge(xp, hp, ya, yb, yc, w_g, gb, woo, next_g, tt=PROMPT_TILE, final=final)
        pa.append(na); pc.append(nc)

        hs2 = hs.reshape(rows, d)
        ya, sa = _mixer_a_sample(hs.reshape(steps, n_seq, d), conv_a_tm, w_a, conv_a_w[l], cab, lng, lnb,
                                 wao, sa, layer=l, nb=SAMPLE_CONV_SEQS)
        qg, kh, v, eb, intra, gz = _mixer_b_sample_proj(hs2, w_b, lb_raw, layer=l, steps=steps)
        tm = lambda a: a.reshape(steps, n_seq, w_bk)
        inter, sh = _mixer_b_sample_state(state_hgrn, tm(qg), tm(kh), tm(v), eb, sh, layer=l,
                                          nb=SAMPLE_STATE_SEQS)
        yb = _mixer_b_sample_out(inter.reshape(rows, w_bv), intra, gz, hgn, wbo)
        yc, nc = _mixer_c_sample(hs2, state_conv_c[l], w_c, conv_c_w[l], wco, steps=steps)
        tile = lambda a: a.reshape(1, rows, d)
        xs, hs = _merge(xs, hs, tile(ya), tile(yb), tile(yc), w_g, gb, woo, next_g, tt=rows, final=final)
        sc.append(nc)

    y_sample = jnp.swapaxes(xs.reshape(steps, n_seq, d), 0, 1)
    return (xp, y_sample, jnp.stack(pa), ph, jnp.stack(pc), jnp.swapaxes(sa, 1, 2), sh, jnp.stack(sc))
```

```python
import functools
from typing import NamedTuple

import numpy as np
import jax
import jax.numpy as jnp
from jax import lax
from jax.experimental import pallas as pl
from jax.experimental.pallas import tpu as pltpu

F32 = jnp.float32
BF16 = jnp.bfloat16

EPS = 1e-6
D_MODEL = 1024
HG_HEADS = 8
HG_DK = 128
HG_DV = 128
CONV_A = 31
CONV_C = 3

SUBLANES = 8
LANES = 128
HALO_A = 32
HALO_C = 8
CONV_ROWS = 64
HG_CHUNK = 64
HG_SLOTS = 4
PROMPT_TILE = 512
PROMPT_TILE_C = 1024
SAMPLE_CONV_SEQS = 32
SAMPLE_STATE_SEQS = 8
VMEM_LIMIT = 56 * 1024 * 1024

NT_DIMS = (((1,), (1,)), ((), ()))
TN_DIMS = (((0,), (0,)), ((), ()))


def _sigmoid(x):
    return 1.0 / (1.0 + jnp.exp(-x))


def _silu(x):
    return x * _sigmoid(x)


def _rms_rows(x, g):
    ms = jnp.mean(x * x, axis=-1, keepdims=True)
    return x * lax.rsqrt(ms + EPS) * g


def _layernorm_rows(x, g, b):
    mu = jnp.mean(x, axis=-1, keepdims=True)
    xc = x - mu
    var = jnp.mean(xc * xc, axis=-1, keepdims=True)
    return xc * lax.rsqrt(var + EPS) * g + b


def _dot(a, b):
    return jnp.dot(a, b, preferred_element_type=F32)


def _params(sem):
    return pltpu.CompilerParams(dimension_semantics=sem, vmem_limit_bytes=VMEM_LIMIT,
                                )


def _full(shape):
    n = len(shape)
    return pl.BlockSpec(shape, lambda *_: (0,) * n)


class _Cols(NamedTuple):
    array: jax.Array
    row0: int
    rows: int
    col0: int
    width: int

    def spec(self):
        return pl.BlockSpec((pl.Element(self.rows), pl.Element(self.width)), lambda *_: (self.row0, self.col0))


def _stacked_slot(prev, n_inputs, out_index):
    if prev is None:
        return [], [], {}
    return [pl.BlockSpec(memory_space=pl.ANY)], [prev], {n_inputs: out_index}


def _mixer_a_kernel(h_ref, w_ref, cw_ref, cb_ref, lng_ref, lnb_ref, wo_ref,
                    ya_ref, na_ref, ubuf, cabuf, shbuf, *, tt, nt):
    t = pl.program_id(1)

    @pl.when(t == 0)
    def _():
        ubuf[0:HALO_A, :] = jnp.zeros((HALO_A, D_MODEL), F32)

    h = h_ref[0]
    gate = _sigmoid(_dot(h, w_ref[:, D_MODEL:2 * D_MODEL]))
    ubuf[HALO_A:HALO_A + tt, :] = _dot(h, w_ref[:, :D_MODEL]) * gate
    gz = _silu(_dot(h, w_ref[:, 2 * D_MODEL:]))

    off = HALO_A - (CONV_A - 1)

    nsh = HALO_A + tt - SUBLANES

    def lane_tile(c, carry):
        cs = pl.ds(pl.multiple_of(c * LANES, LANES), LANES)
        for r in range(1, SUBLANES):
            shbuf[r - 1, :, cs] = ubuf[r:r + nsh, cs]
        for rb in range(tt // CONV_ROWS):
            r0 = rb * CONV_ROWS
            acc = jnp.broadcast_to(cb_ref[:, cs], (CONV_ROWS, LANES))
            for j in range(CONV_A):
                base, r = divmod(off + j, SUBLANES)
                lo = r0 + base * SUBLANES
                win = ubuf[lo:lo + CONV_ROWS, cs] if r == 0 else shbuf[r - 1, lo:lo + CONV_ROWS, cs]
                acc = acc + cw_ref[j:j + 1, cs] * win
            cabuf[r0:r0 + CONV_ROWS, cs] = acc
        return carry

    lax.fori_loop(0, D_MODEL // LANES, lane_tile, 0)

    @pl.when(t == nt - 1)
    def _():
        na_ref[0] = ubuf[tt + off:tt + HALO_A, :]

    ubuf[0:HALO_A, :] = ubuf[tt:tt + HALO_A, :]

    y = _silu(_layernorm_rows(cabuf[...], lng_ref[...], lnb_ref[...])) * gz
    ya_ref[0] = _dot(y.astype(BF16), wo_ref[...])


def _mixer_a_prompt(h, w, cw, cb, lng, lnb, wo, tt):
    bsz, seq, d = h.shape
    nt = seq // tt
    kern = functools.partial(_mixer_a_kernel, tt=tt, nt=nt)
    return pl.pallas_call(
        kern,
        grid=(bsz, nt),
        in_specs=[
            pl.BlockSpec((1, tt, d), lambda b, t: (b, t, 0)),
            w.spec(), _full(cw.shape), _full((1, d)), _full((1, d)), _full((1, d)),
            wo.spec(),
        ],
        out_specs=[
            pl.BlockSpec((1, tt, d), lambda b, t: (b, t, 0)),
            pl.BlockSpec((1, CONV_A - 1, d), lambda b, t: (b, 0, 0)),
        ],
        out_shape=[
            jax.ShapeDtypeStruct((bsz, seq, d), F32),
            jax.ShapeDtypeStruct((bsz, CONV_A - 1, d), F32),
        ],
        scratch_shapes=[
            pltpu.VMEM((HALO_A + tt, d), F32),
            pltpu.VMEM((tt, d), F32),
            pltpu.VMEM((SUBLANES - 1, HALO_A + tt - SUBLANES, d), F32),
        ],
        compiler_params=_params(("arbitrary", "arbitrary")),
        name="mixer_a_prompt",
    )(h, w.array, cw, cb, lng, lnb, wo.array)


def _mixer_a_sample_kernel(h_ref, st_ref, w_ref, cw_ref, cb_ref, lng_ref, lnb_ref, wo_ref, *rest,
                           steps, nb):
    ya_ref, na_ref = rest[-2:]
    h = h_ref[...].reshape(steps * nb, D_MODEL)
    p = _dot(h, w_ref[...])
    u = p[:, :D_MODEL] * _sigmoid(p[:, D_MODEL:2 * D_MODEL])
    gz = _silu(p[:, 2 * D_MODEL:])
    past = CONV_A - 1

    acc = [jnp.broadcast_to(cb_ref[...], (nb, D_MODEL)) for _ in range(steps)]
    for i in range(past + steps):
        slab = st_ref[i] if i < past else u[(i - past) * nb:(i - past + 1) * nb, :]
        for s in range(steps):
            j = i - s
            if 0 <= j < CONV_A:
                acc[s] = acc[s] + cw_ref[j:j + 1, :] * slab
    ca = jnp.concatenate(acc, axis=0)

    na_ref[0:past - steps] = st_ref[steps:past]
    for s in range(steps):
        na_ref[past - steps + s] = u[s * nb:(s + 1) * nb, :]

    y = _silu(_layernorm_rows(ca, lng_ref[...], lnb_ref[...])) * gz
    ya_ref[...] = _dot(y.astype(BF16), wo_ref[...]).reshape(steps, nb, D_MODEL)


def _mixer_a_sample(h_tm, states, w, cw, cb, lng, lnb, wo, prev, layer, nb):
    steps, n, d = h_tm.shape
    kern = functools.partial(_mixer_a_sample_kernel, steps=steps, nb=nb)
    st = pl.BlockSpec((None, CONV_A - 1, nb, d), lambda i: (layer, 0, i, 0))
    operands = [h_tm, states, w.array, cw, cb, lng, lnb, wo.array]
    slot_specs, slot_args, aliases = _stacked_slot(prev, len(operands), 1)
    return pl.pallas_call(
        kern,
        grid=(n // nb,),
        in_specs=[
            pl.BlockSpec((steps, nb, d), lambda i: (0, i, 0)),
            st,
            w.spec(), _full(cw.shape), _full((1, d)), _full((1, d)), _full((1, d)),
            wo.spec(),
        ] + slot_specs,
        out_specs=[pl.BlockSpec((steps, nb, d), lambda i: (0, i, 0)), st],
        out_shape=[jax.ShapeDtypeStruct((steps, n, d), F32), jax.ShapeDtypeStruct(states.shape, F32)],
        input_output_aliases=aliases,
        compiler_params=_params(("arbitrary",)),
        name="mixer_a_sample",
    )(*operands, *slot_args)


def _mixer_c_kernel(h_ref, w_ref, cw_ref, wo_ref, yc_ref, nc_ref, vbuf, *, tt, nt):
    t = pl.program_id(1)

    @pl.when(t == 0)
    def _():
        vbuf[0:HALO_C, :] = jnp.zeros((HALO_C, D_MODEL), F32)

    h = h_ref[0]
    vbuf[HALO_C:HALO_C + tt, :] = (_dot(h, w_ref[:, D_MODEL:2 * D_MODEL])
                                   * _dot(h, w_ref[:, 2 * D_MODEL:3 * D_MODEL]))
    off = HALO_C - (CONV_C - 1)
    cc = cw_ref[0:1, :] * vbuf[off:off + tt, :]
    for j in range(1, CONV_C):
        cc = cc + cw_ref[j:j + 1, :] * vbuf[off + j:off + j + tt, :]

    @pl.when(t == nt - 1)
    def _():
        nc_ref[0] = vbuf[tt + off:tt + HALO_C, :]

    vbuf[0:HALO_C, :] = vbuf[tt:tt + HALO_C, :]
    y = _dot(h, w_ref[:, :D_MODEL]) * cc * _silu(_dot(h, w_ref[:, 3 * D_MODEL:]))
    yc_ref[0] = _dot(y.astype(BF16), wo_ref[...])


def _mixer_c_prompt(h, w, cw, wo, tt):
    bsz, seq, d = h.shape
    nt = seq // tt
    kern = functools.partial(_mixer_c_kernel, tt=tt, nt=nt)
    return pl.pallas_call(
        kern,
        grid=(bsz, nt),
        in_specs=[
            pl.BlockSpec((1, tt, d), lambda b, t: (b, t, 0)),
            w.spec(), _full(cw.shape), wo.spec(),
        ],
        out_specs=[
            pl.BlockSpec((1, tt, d), lambda b, t: (b, t, 0)),
            pl.BlockSpec((1, CONV_C - 1, d), lambda b, t: (b, 0, 0)),
        ],
        out_shape=[
            jax.ShapeDtypeStruct((bsz, seq, d), F32),
            jax.ShapeDtypeStruct((bsz, CONV_C - 1, d), F32),
        ],
        scratch_shapes=[pltpu.VMEM((HALO_C + tt, d), F32)],
        compiler_params=_params(("arbitrary", "arbitrary")),
        name="mixer_c_prompt",
    )(h, w.array, cw, wo.array)


def _mixer_c_sample_kernel(h_ref, st_ref, w_ref, cw_ref, wo_ref, yc_ref, nc_ref, *, steps, n):
    h = h_ref[...]
    p = _dot(h, w_ref[...])
    v = p[:, D_MODEL:2 * D_MODEL] * p[:, 2 * D_MODEL:3 * D_MODEL]
    past = CONV_C - 1
    slabs = [st_ref[:, i, :] for i in range(past)] + [v[s * n:(s + 1) * n, :] for s in range(steps)]
    rows = []
    for s in range(steps):
        cc = cw_ref[0:1, :] * slabs[s]
        for j in range(1, CONV_C):
            cc = cc + cw_ref[j:j + 1, :] * slabs[s + j]
        rows.append(cc)
    cc = jnp.concatenate(rows, axis=0)
    for i in range(past):
        nc_ref[:, i, :] = slabs[steps + i]
    y = p[:, :D_MODEL] * cc * _silu(p[:, 3 * D_MODEL:])
    yc_ref[...] = _dot(y.astype(BF16), wo_ref[...])


def _mixer_c_sample(h2, state, w, cw, wo, steps):
    rows, d = h2.shape
    n = rows // steps
    kern = functools.partial(_mixer_c_sample_kernel, steps=steps, n=n)
    return pl.pallas_call(
        kern,
        grid=(1,),
        in_specs=[_full((rows, d)), _full(state.shape), w.spec(), _full(cw.shape), wo.spec()],
        out_specs=[_full((rows, d)), _full(state.shape)],
        out_shape=[jax.ShapeDtypeStruct((rows, d), F32), jax.ShapeDtypeStruct(state.shape, F32)],
        compiler_params=_params(("arbitrary",)),
        name="mixer_c_sample",
    )(h2, state, w.array, cw, wo.array)


def _merge_kernel(x_ref, h_ref, ya_ref, yb_ref, yc_ref, w_ref, gb_ref, wo_ref, ng_ref, *out_refs, final):
    h = h_ref[0]
    m = None
    for i, y_ref in enumerate((ya_ref, yb_ref, yc_ref)):
        cs = slice(i * D_MODEL, (i + 1) * D_MODEL)
        term = _sigmoid(_dot(h, w_ref[:, cs]) + gb_ref[:, cs]) * y_ref[0]
        m = term if m is None else m + term
    out = x_ref[0] + _dot(m.astype(BF16), wo_ref[...])
    normed = _rms_rows(out, ng_ref[...])
    if final:
        out_refs[0][0] = normed
    else:
        out_refs[0][0] = out
        out_refs[1][0] = normed.astype(BF16)


def _merge(x, h, ya, yb, yc, w, gate_b, wo, next_g, tt, final):
    bsz, seq, d = x.shape
    tile = pl.BlockSpec((1, tt, d), lambda b, t: (b, t, 0))
    kern = functools.partial(_merge_kernel, final=final)
    out_shape = [jax.ShapeDtypeStruct((bsz, seq, d), F32)]
    if not final:
        out_shape.append(jax.ShapeDtypeStruct((bsz, seq, d), BF16))
    outs = pl.pallas_call(
        kern,
        grid=(bsz, seq // tt),
        in_specs=[tile, tile, tile, tile, tile, w.spec(), _full(gate_b.shape), wo.spec(),
                  _full((1, d))],
        out_specs=[tile] * len(out_shape),
        out_shape=out_shape,
        compiler_params=_params(("arbitrary", "arbitrary")),
        name="merge",
    )(x, h, ya, yb, yc, w.array, gate_b, wo.array, next_g)
    return (outs[0], None) if final else (outs[0], outs[1])


def _rmsnorm_kernel(x_ref, g_ref, h_ref):
    h_ref[0] = _rms_rows(x_ref[0], g_ref[...]).astype(BF16)


def _rmsnorm_bf16(x, g, tt):
    bsz, seq, d = x.shape
    tile = pl.BlockSpec((1, tt, d), lambda b, t: (b, t, 0))
    return pl.pallas_call(
        _rmsnorm_kernel,
        grid=(bsz, seq // tt),
        in_specs=[tile, _full((1, d))],
        out_specs=tile,
        out_shape=jax.ShapeDtypeStruct((bsz, seq, d), BF16),
        compiler_params=_params(("arbitrary", "arbitrary")),
        name="rmsnorm_bf16",
    )(x, g)


def _lower_bound(lbraw_ref, layer):
    depth = lbraw_ref.shape[0]
    rows = [lbraw_ref[i:i + 1, :] for i in range(depth)]
    mx = functools.reduce(jnp.maximum, rows)
    ex = [jnp.exp(r - mx) for r in rows]
    tot = functools.reduce(lambda a, b: a + b, ex)
    csum = []
    run = None
    for e in ex:
        sm = e / tot
        run = sm if run is None else run + sm
        csum.append(run)
    return csum[layer] - csum[0]


def _forget_terms(f_pre, lb):
    e = jnp.exp(-jnp.abs(f_pre))
    r = 1.0 / (1.0 + e)
    er = e * r
    pos = f_pre > 0.0
    log_sig = jnp.minimum(f_pre, 0.0) + jnp.log(r)
    one_m_lb = 1.0 - lb
    f = lb + one_m_lb * jnp.where(pos, r, er)
    logf = jnp.where(lb > 0.0, jnp.log(f), log_sig)
    k = one_m_lb * jnp.where(pos, er, r)
    return logf, k


def _split3(x):
    hi = x.astype(BF16)
    r1 = x - hi.astype(F32)
    mid = r1.astype(BF16)
    lo = (r1 - mid.astype(F32)).astype(BF16)
    return hi, mid, lo


def _hgrn_chunk_consts(c):
    levels = c.bit_length() - 1
    t = np.arange(c)
    col = t[None, :]
    row = t[:, None]
    blocks = [col <= row, col > row]
    masks = []
    for j in range(levels):
        half = 1 << j
        p = ((t >> (j + 1)) << (j + 1)) + half
        tgt = ((t >> j) & 1) == 1
        seg_t = (col >= p[:, None]) & (col <= row)
        seg_s = (col > row) & (col <= p[:, None] - 1)
        blocks.append(np.where(tgt[:, None], seg_t, seg_s))
        same = (row >> (j + 1)) == (col >> (j + 1))
        masks.append(same & tgt[:, None] & (~tgt)[None, :])
    assert (np.sum(np.stack(masks), axis=0) == (col < row)).all()
    seg = np.concatenate(blocks, axis=0).astype(np.float32)
    seg3 = np.concatenate([seg, seg, seg], axis=1)
    pair_masks = np.tile(np.stack(masks).astype(np.float32), (1, 1, 2))
    return jnp.asarray(seg3, BF16), jnp.asarray(pair_masks), levels


def _mixer_b_kernel(h_ref, w_ref, lbraw_ref, hgn_ref, wo_ref, seg_ref, mask_ref, *rest,
                    tt, nt, layer, levels):
    yb_ref, nh_ref, q_s, k_s, lf_s, v_s, gz_s, y_s, dec2_s, att2_s, st_s = rest[-11:]
    t = pl.program_id(1)
    c = HG_CHUNK
    w_bk = HG_HEADS * HG_DK

    @pl.when(t == 0)
    def _():
        st_s[...] = jnp.zeros(st_s.shape, F32)

    h = h_ref[0]
    logf, kk = _forget_terms(_dot(h, w_ref[:, w_bk:2 * w_bk]), _lower_bound(lbraw_ref, layer))
    lf_s[...] = logf
    k_s[...] = kk
    gz_s[...] = _silu(_dot(h, w_ref[:, 3 * w_bk:]))
    q_s[...] = _dot(h, w_ref[:, :w_bk])
    v_s[...] = _dot(h, w_ref[:, 2 * w_bk:3 * w_bk]).astype(BF16)

    def chunk(ci, dec_s, att_s):
        rows = pl.ds(pl.multiple_of(ci * c, c), c)
        hi, mid, lo = _split3(lf_s[rows, :])
        expo = _dot(seg_ref[...], jnp.concatenate([hi, mid, lo], axis=0))
        dec_s[...] = jnp.exp(expo)
        zrow = jnp.zeros((c, HG_DK), BF16)
        zblk = jnp.zeros((HG_DV, HG_DK), BF16)

        def blockdiag(m, z):
            return jnp.concatenate([jnp.concatenate([m[:, :HG_DK], z], axis=1),
                                    jnp.concatenate([z, m[:, HG_DK:]], axis=1)], axis=0)

        for pr in range(HG_HEADS // 2):
            cs2 = slice(2 * pr * HG_DK, 2 * (pr + 1) * HG_DK)
            q2 = q_s[rows, cs2]
            k2 = k_s[rows, cs2]
            att = None
            for j in range(levels):
                gj = dec_s[(2 + j) * c:(3 + j) * c, cs2]
                kj = (k2 * gj).astype(BF16)
                pj = lax.dot_general((q2 * gj).astype(BF16), blockdiag(kj, zrow), NT_DIMS,
                                     preferred_element_type=F32) * mask_ref[j]
                att = pj if att is None else att + pj
            att_s[pr] = att.astype(BF16)

        for pr in range(HG_HEADS // 2):
            cs2 = slice(2 * pr * HG_DK, 2 * (pr + 1) * HG_DK)
            q2 = q_s[rows, cs2]
            k2 = k_s[rows, cs2]
            v2 = v_s[rows, cs2]
            qg = (q2 * dec_s[0:c, cs2]).astype(BF16)
            st_t = jnp.concatenate(
                [jnp.concatenate([st_s[2 * pr].astype(BF16), zblk], axis=1),
                 jnp.concatenate([zblk, st_s[2 * pr + 1].astype(BF16)], axis=1)], axis=0)
            o2 = lax.dot_general(qg, st_t, NT_DIMS, preferred_element_type=F32)
            o2 = o2 + _dot(att_s[pr], blockdiag(v2, zrow))
            qk = q2 * k2
            for i in range(2):
                hd = 2 * pr + i
                cs = slice(hd * HG_DK, (hd + 1) * HG_DK)
                hs = slice(i * HG_DK, (i + 1) * HG_DK)
                o = o2[:, hs] + jnp.sum(qk[:, hs], axis=-1, keepdims=True) * v2[:, hs].astype(F32)
                o = o * lax.rsqrt(jnp.mean(o * o, axis=-1, keepdims=True) + EPS) * hgn_ref[:, cs]
                y_s[rows, cs] = (o * gz_s[rows, cs]).astype(BF16)

        for pr in range(HG_HEADS // 2):
            cs2 = slice(2 * pr * HG_DK, 2 * (pr + 1) * HG_DK)
            khat = (k_s[rows, cs2] * dec_s[c:2 * c, cs2]).astype(BF16)
            upd = lax.dot_general(v_s[rows, cs2], khat, TN_DIMS, preferred_element_type=F32)
            last = dec_s[c - 1:c, cs2]
            for i in range(2):
                hs = slice(i * HG_DK, (i + 1) * HG_DK)
                st_s[2 * pr + i] = st_s[2 * pr + i] * last[:, hs] + upd[hs, hs]

    def chunk_group(cg, carry):
        for e in range(HG_SLOTS):
            chunk(HG_SLOTS * cg + e, dec2_s.at[e], att2_s.at[e])
        return carry

    lax.fori_loop(0, tt // (HG_SLOTS * c), chunk_group, 0)
    yb_ref[0] = _dot(y_s[...], wo_ref[...])

    @pl.when(t == nt - 1)
    def _():
        for hd in range(HG_HEADS):
            nh_ref[0, hd] = st_s[hd].T


def _mixer_b_prompt(h, w, lb_raw, hgn, wo, prev, layer, depth, tt):
    bsz, seq, d = h.shape
    nt = seq // tt
    seg3, masks, levels = _hgrn_chunk_consts(HG_CHUNK)
    w_bk = HG_HEADS * HG_DK
    kern = functools.partial(_mixer_b_kernel, tt=tt, nt=nt, layer=layer, levels=levels)
    operands = [h, w.array, lb_raw, hgn, wo.array, seg3, masks]
    slot_specs, slot_args, aliases = _stacked_slot(prev, len(operands), 1)
    return pl.pallas_call(
        kern,
        grid=(bsz, nt),
        in_specs=[
            pl.BlockSpec((1, tt, d), lambda b, t: (b, t, 0)),
            w.spec(), _full(lb_raw.shape), _full((1, w_bk)), wo.spec(),
            _full(seg3.shape), _full(masks.shape),
        ] + slot_specs,
        out_specs=[
            pl.BlockSpec((1, tt, d), lambda b, t: (b, t, 0)),
            pl.BlockSpec((None, 1, HG_HEADS, HG_DK, HG_DV), lambda b, t: (layer, b, 0, 0, 0)),
        ],
        out_shape=[
            jax.ShapeDtypeStruct((bsz, seq, d), F32),
            jax.ShapeDtypeStruct((depth, bsz, HG_HEADS, HG_DK, HG_DV), F32),
        ],
        input_output_aliases=aliases,
        scratch_shapes=[
            pltpu.VMEM((tt, w_bk), F32),
            pltpu.VMEM((tt, w_bk), F32),
            pltpu.VMEM((tt, w_bk), F32),
            pltpu.VMEM((tt, w_bk), BF16),
            pltpu.VMEM((tt, w_bk), F32),
            pltpu.VMEM((tt, w_bk), BF16),
            pltpu.VMEM((HG_SLOTS, (levels + 2) * HG_CHUNK, w_bk), F32),
            pltpu.VMEM((HG_SLOTS, HG_HEADS // 2, HG_CHUNK, 2 * HG_CHUNK), BF16),
            pltpu.VMEM((HG_HEADS, HG_DV, HG_DK), F32),
        ],
        compiler_params=_params(("arbitrary", "arbitrary")),
        name="mixer_b_prompt",
    )(*operands, *slot_args)


def _mixer_b_sample_proj_kernel(h_ref, w_ref, lbraw_ref,
                                qg_ref, kh_ref, v_ref, eb_ref, oi_ref, gz_ref, *, steps, n, layer):
    w_bk = HG_HEADS * HG_DK
    h = h_ref[...]
    p = _dot(h, w_ref[...])
    logf, kk = _forget_terms(p[:, w_bk:2 * w_bk], _lower_bound(lbraw_ref, layer))
    q = p[:, :w_bk]
    v = p[:, 2 * w_bk:3 * w_bk]
    gz_ref[...] = _silu(p[:, 3 * w_bk:])
    v_ref[...] = v.astype(BF16)

    rs = lambda a, s: a[s * n:(s + 1) * n, :]
    bc = []
    for s in range(steps):
        bc.append(rs(logf, s) if s == 0 else bc[-1] + rs(logf, s))
    eb_ref[...] = jnp.exp(bc[-1])
    ones = jnp.ones((HG_DK, HG_DK), BF16)
    for s in range(steps):
        qg_ref[s * n:(s + 1) * n, :] = (rs(q, s) * jnp.exp(bc[s])).astype(BF16)
        kh_ref[s * n:(s + 1) * n, :] = (rs(kk, s) * jnp.exp(bc[-1] - bc[s])).astype(BF16)
        acc = jnp.zeros((n, w_bk), F32)
        for r in range(s + 1):
            pr = rs(q, s) * rs(kk, r)
            if r < s:
                pr = pr * jnp.exp(bc[s] - bc[r])
            pr = pr.astype(BF16)
            att = jnp.concatenate(
                [_dot(pr[:, hd * HG_DK:(hd + 1) * HG_DK], ones) for hd in range(HG_HEADS)], axis=1)
            acc = acc + att * rs(v, r)
        oi_ref[s * n:(s + 1) * n, :] = acc


def _mixer_b_sample_proj(h2, w, lb_raw, layer, steps):
    rows, d = h2.shape
    n = rows // steps
    w_bk = HG_HEADS * HG_DK
    kern = functools.partial(_mixer_b_sample_proj_kernel, steps=steps, n=n, layer=layer)
    return pl.pallas_call(
        kern,
        grid=(1,),
        in_specs=[_full((rows, d)), w.spec(), _full(lb_raw.shape)],
        out_specs=[_full((rows, w_bk)), _full((rows, w_bk)), _full((rows, w_bk)), _full((n, w_bk)),
                   _full((rows, w_bk)), _full((rows, w_bk))],
        out_shape=[
            jax.ShapeDtypeStruct((rows, w_bk), BF16),
            jax.ShapeDtypeStruct((rows, w_bk), BF16),
            jax.ShapeDtypeStruct((rows, w_bk), BF16),
            jax.ShapeDtypeStruct((n, w_bk), F32),
            jax.ShapeDtypeStruct((rows, w_bk), F32),
            jax.ShapeDtypeStruct((rows, w_bk), F32),
        ],
        compiler_params=_params(("arbitrary",)),
        name="mixer_b_sample_proj",
    )(h2, w.array, lb_raw)


def _mixer_b_sample_state_kernel(st_ref, qg_ref, kh_ref, v_ref, eb_ref, *rest, steps, nb):
    io_ref, ns_ref = rest[-2:]
    pad = SUBLANES - 3 - steps
    ones3 = jnp.ones((3, HG_DV), BF16)
    for i in range(nb):
        for hd in range(HG_HEADS):
            cs = slice(hd * HG_DK, (hd + 1) * HG_DK)
            s = st_ref[i, hd]
            io_ref[:, i, cs] = _dot(qg_ref[:, i, cs], s.astype(BF16))
            hi, mid, lo = _split3(eb_ref[i:i + 1, cs])
            lhs = jnp.concatenate([hi, mid, lo, kh_ref[:, i, cs], jnp.zeros((pad, HG_DK), BF16)], axis=0)
            top = jnp.concatenate([ones3, jnp.zeros((3, HG_DV), BF16)], axis=1)
            mid_rows = jnp.concatenate([jnp.zeros((steps, HG_DV), BF16), v_ref[:, i, cs]], axis=1)
            rhs = jnp.concatenate([top, mid_rows, jnp.zeros((pad, 2 * HG_DV), BF16)], axis=0)
            r = lax.dot_general(lhs, rhs, TN_DIMS, preferred_element_type=F32)
            ns_ref[i, hd] = r[:, :HG_DV] * s + r[:, HG_DV:]


def _mixer_b_sample_state(states, qg, kh, v, eb, prev, layer, nb):
    steps, n, w_bk = qg.shape
    kern = functools.partial(_mixer_b_sample_state_kernel, steps=steps, nb=nb)
    tok = pl.BlockSpec((steps, nb, w_bk), lambda i: (0, i, 0))
    st = pl.BlockSpec((None, nb, HG_HEADS, HG_DK, HG_DV), lambda i: (layer, i, 0, 0, 0))
    operands = [states, qg, kh, v, eb]
    slot_specs, slot_args, aliases = _stacked_slot(prev, len(operands), 1)
    return pl.pallas_call(
        kern,
        grid=(n // nb,),
        in_specs=[st, tok, tok, tok, pl.BlockSpec((nb, w_bk), lambda i: (i, 0))] + slot_specs,
        out_specs=[tok, st],
        out_shape=[jax.ShapeDtypeStruct((steps, n, w_bk), F32), jax.ShapeDtypeStruct(states.shape, F32)],
        input_output_aliases=aliases,
        compiler_params=_params(("arbitrary",)),
        name="mixer_b_sample_state",
    )(*operands, *slot_args)


def _mixer_b_sample_out_kernel(io_ref, oi_ref, gz_ref, hgn_ref, wo_ref, yb_ref):
    o = io_ref[...] + oi_ref[...]
    parts = []
    for hd in range(HG_HEADS):
        oh = o[:, hd * HG_DV:(hd + 1) * HG_DV]
        parts.append(oh * lax.rsqrt(jnp.mean(oh * oh, axis=-1, keepdims=True) + EPS))
    y = jnp.concatenate(parts, axis=1) * hgn_ref[...] * gz_ref[...]
    yb_ref[...] = _dot(y.astype(BF16), wo_ref[...])


def _mixer_b_sample_out(inter, intra, gz, hgn, wo):
    rows, w_bv = inter.shape
    return pl.pallas_call(
        _mixer_b_sample_out_kernel,
        grid=(1,),
        in_specs=[_full((rows, w_bv))] * 3 + [_full((1, w_bv)), wo.spec()],
        out_specs=_full((rows, wo.width)),
        out_shape=jax.ShapeDtypeStruct((rows, wo.width), F32),
        compiler_params=_params(("arbitrary",)),
        name="mixer_b_sample_out",
    )(inter, intra, gz, hgn, wo.array)


def kernel(x_prompt, x_sample, state_conv_a, state_hgrn, state_conv_c, norm_g, w_in, gate_b, conv_a_w,
           conv_a_b, ln_g, ln_b, w_a_out, lb_raw, hg_norm_g, w_b_out, conv_c_w, w_c_out, w_o, final_norm_g):
    depth = w_in.shape[0]
    d = x_prompt.shape[-1]
    n_seq, steps, _ = x_sample.shape
    w_bk = HG_HEADS * HG_DK
    w_bv = HG_HEADS * HG_DV
    col_a = 3 * d
    col_b = col_a + 2 * w_bk + 2 * w_bv
    col_c = col_b + 4 * d
    row = lambda a: a.reshape(1, -1)
    fin_g = row(final_norm_g)

    rows = steps * n_seq
    xp = x_prompt
    xs = jnp.swapaxes(x_sample, 0, 1).reshape(1, rows, d)
    hp = _rmsnorm_bf16(xp, row(norm_g[0]), tt=xp.shape[1])
    hs = _rmsnorm_bf16(xs, row(norm_g[0]), tt=rows)
    conv_a_tm = jnp.swapaxes(state_conv_a, 1, 2)
    pa, pc, sc = [], [], []
    ph = sa = sh = None
    stack_bf16 = lambda w: w.astype(BF16).reshape(depth * w.shape[1], w.shape[2])
    w_in_bf = stack_bf16(w_in)
    out_bf = [stack_bf16(w) for w in (w_a_out, w_b_out, w_c_out, w_o)]
    for l in range(depth):
        cols = lambda a, b: _Cols(w_in_bf, l * d, d, a, b - a)
        w_a, w_b, w_c, w_g = (cols(0, col_a), cols(col_a, col_b), cols(col_b, col_c),
                              cols(col_c, w_in.shape[-1]))
        wao, wbo, wco, woo = (_Cols(w, l * w.shape[0] // depth, w.shape[0] // depth, 0, w.shape[1])
                              for w in out_bf)
        gb = row(gate_b[l])
        cab, lng, lnb, hgn = row(conv_a_b[l]), row(ln_g[l]), row(ln_b[l]), row(hg_norm_g[l])
        final = l == depth - 1
        next_g = fin_g if final else row(norm_g[l + 1])

        ya, na = _mixer_a_prompt(hp, w_a, conv_a_w[l], cab, lng, lnb, wao, tt=PROMPT_TILE)
        yb, ph = _mixer_b_prompt(hp, w_b, lb_raw, hgn, wbo, ph, layer=l, depth=depth, tt=PROMPT_TILE)
        yc, nc = _mixer_c_prompt(hp, w_c, conv_c_w[l], wco, tt=PROMPT_TILE_C)
        xp, hp = _merge(xp, hp, ya, yb, yc, w_g, gb, woo, next_g, tt=PROMPT_TILE, final=final)
        pa.append(na); pc.append(nc)

        hs2 = hs.reshape(rows, d)
        ya, sa = _mixer_a_sample(hs.reshape(steps, n_seq, d), conv_a_tm, w_a, conv_a_w[l], cab, lng, lnb,
                                 wao, sa, layer=l, nb=SAMPLE_CONV_SEQS)
        qg, kh, v, eb, intra, gz = _mixer_b_sample_proj(hs2, w_b, lb_raw, layer=l, steps=steps)
        tm = lambda a: a.reshape(steps, n_seq, w_bk)
        inter, sh = _mixer_b_sample_state(state_hgrn, tm(qg), tm(kh), tm(v), eb, sh, layer=l,
                                          nb=SAMPLE_STATE_SEQS)
        yb = _mixer_b_sample_out(inter.reshape(rows, w_bv), intra, gz, hgn, wbo)
        yc, nc = _mixer_c_sample(hs2, state_conv_c[l], w_c, conv_c_w[l], wco, steps=steps)
        tile = lambda a: a.reshape(1, rows, d)
        xs, hs = _merge(xs, hs, tile(ya), tile(yb), tile(yc), w_g, gb, woo, next_g, tt=rows, final=final)
        sc.append(nc)

    y_sample = jnp.swapaxes(xs.reshape(steps, n_seq, d), 0, 1)
    return (xp, y_sample, jnp.stack(pa), ph, jnp.stack(pc), jnp.swapaxes(sa, 1, 2), sh, jnp.stack(sc))
```

```python
import functools
from typing import NamedTuple

import numpy as np
import jax
import jax.numpy as jnp
from jax import lax
from jax.experimental import pallas as pl
from jax.experimental.pallas import tpu as pltpu

F32 = jnp.float32
BF16 = jnp.bfloat16

EPS = 1e-6
D_MODEL = 1024
HG_HEADS = 8
HG_DK = 128
HG_DV = 128
CONV_A = 31
CONV_C = 3

SUBLANES = 8
LANES = 128
HALO_A = 32
HALO_C = 8
CONV_ROWS = 64
HG_CHUNK = 64
HG_SLOTS = 8
PROMPT_TILE = 512
PROMPT_TILE_C = 1024
SAMPLE_CONV_SEQS = 32
SAMPLE_STATE_SEQS = 8
VMEM_LIMIT = 56 * 1024 * 1024

NT_DIMS = (((1,), (1,)), ((), ()))
TN_DIMS = (((0,), (0,)), ((), ()))


def _sigmoid(x):
    return 1.0 / (1.0 + jnp.exp(-x))


def _silu(x):
    return x * _sigmoid(x)


def _rms_rows(x, g):
    ms = jnp.mean(x * x, axis=-1, keepdims=True)
    return x * lax.rsqrt(ms + EPS) * g


def _layernorm_rows(x, g, b):
    mu = jnp.mean(x, axis=-1, keepdims=True)
    xc = x - mu
    var = jnp.mean(xc * xc, axis=-1, keepdims=True)
    return xc * lax.rsqrt(var + EPS) * g + b


def _dot(a, b):
    return jnp.dot(a, b, preferred_element_type=F32)


def _params(sem):
    return pltpu.CompilerParams(dimension_semantics=sem, vmem_limit_bytes=VMEM_LIMIT,
                                )


def _full(shape):
    n = len(shape)
    return pl.BlockSpec(shape, lambda *_: (0,) * n)


class _Cols(NamedTuple):
    array: jax.Array
    row0: int
    rows: int
    col0: int
    width: int

    def spec(self):
        return pl.BlockSpec((pl.Element(self.rows), pl.Element(self.width)), lambda *_: (self.row0, self.col0))


def _stacked_slot(prev, n_inputs, out_index):
    if prev is None:
        return [], [], {}
    return [pl.BlockSpec(memory_space=pl.ANY)], [prev], {n_inputs: out_index}


def _mixer_a_kernel(h_ref, w_ref, cw_ref, cb_ref, lng_ref, lnb_ref, wo_ref,
                    ya_ref, na_ref, ubuf, cabuf, shbuf, *, tt, nt):
    t = pl.program_id(1)

    @pl.when(t == 0)
    def _():
        ubuf[0:HALO_A, :] = jnp.zeros((HALO_A, D_MODEL), F32)

    h = h_ref[0]
    gate = _sigmoid(_dot(h, w_ref[:, D_MODEL:2 * D_MODEL]))
    ubuf[HALO_A:HALO_A + tt, :] = _dot(h, w_ref[:, :D_MODEL]) * gate
    gz = _silu(_dot(h, w_ref[:, 2 * D_MODEL:]))

    off = HALO_A - (CONV_A - 1)

    nsh = HALO_A + tt - SUBLANES

    def lane_tile(c, carry):
        cs = pl.ds(pl.multiple_of(c * LANES, LANES), LANES)
        for r in range(1, SUBLANES):
            shbuf[r - 1, :, cs] = ubuf[r:r + nsh, cs]
        for rb in range(tt // CONV_ROWS):
            r0 = rb * CONV_ROWS
            acc = jnp.broadcast_to(cb_ref[:, cs], (CONV_ROWS, LANES))
            for j in range(CONV_A):
                base, r = divmod(off + j, SUBLANES)
                lo = r0 + base * SUBLANES
                win = ubuf[lo:lo + CONV_ROWS, cs] if r == 0 else shbuf[r - 1, lo:lo + CONV_ROWS, cs]
                acc = acc + cw_ref[j:j + 1, cs] * win
            cabuf[r0:r0 + CONV_ROWS, cs] = acc
        return carry

    lax.fori_loop(0, D_MODEL // LANES, lane_tile, 0, unroll=True)

    @pl.when(t == nt - 1)
    def _():
        na_ref[0] = ubuf[tt + off:tt + HALO_A, :]

    ubuf[0:HALO_A, :] = ubuf[tt:tt + HALO_A, :]

    y = _silu(_layernorm_rows(cabuf[...], lng_ref[...], lnb_ref[...])) * gz
    ya_ref[0] = _dot(y.astype(BF16), wo_ref[...])


def _mixer_a_prompt(h, w, cw, cb, lng, lnb, wo, tt):
    bsz, seq, d = h.shape
    nt = seq // tt
    kern = functools.partial(_mixer_a_kernel, tt=tt, nt=nt)
    return pl.pallas_call(
        kern,
        grid=(bsz, nt),
        in_specs=[
            pl.BlockSpec((1, tt, d), lambda b, t: (b, t, 0)),
            w.spec(), _full(cw.shape), _full((1, d)), _full((1, d)), _full((1, d)),
            wo.spec(),
        ],
        out_specs=[
            pl.BlockSpec((1, tt, d), lambda b, t: (b, t, 0)),
            pl.BlockSpec((1, CONV_A - 1, d), lambda b, t: (b, 0, 0)),
        ],
        out_shape=[
            jax.ShapeDtypeStruct((bsz, seq, d), F32),
            jax.ShapeDtypeStruct((bsz, CONV_A - 1, d), F32),
        ],
        scratch_shapes=[
            pltpu.VMEM((HALO_A + tt, d), F32),
            pltpu.VMEM((tt, d), F32),
            pltpu.VMEM((SUBLANES - 1, HALO_A + tt - SUBLANES, d), F32),
        ],
        compiler_params=_params(("arbitrary", "arbitrary")),
        name="mixer_a_prompt",
    )(h, w.array, cw, cb, lng, lnb, wo.array)


def _mixer_a_sample_kernel(h_ref, st_ref, w_ref, cw_ref, cb_ref, lng_ref, lnb_ref, wo_ref, *rest,
                           steps, nb):
    ya_ref, na_ref = rest[-2:]
    h = h_ref[...].reshape(steps * nb, D_MODEL)
    p = _dot(h, w_ref[...])
    u = p[:, :D_MODEL] * _sigmoid(p[:, D_MODEL:2 * D_MODEL])
    gz = _silu(p[:, 2 * D_MODEL:])
    past = CONV_A - 1

    acc = [jnp.broadcast_to(cb_ref[...], (nb, D_MODEL)) for _ in range(steps)]
    for i in range(past + steps):
        slab = st_ref[i] if i < past else u[(i - past) * nb:(i - past + 1) * nb, :]
        for s in range(steps):
            j = i - s
            if 0 <= j < CONV_A:
                acc[s] = acc[s] + cw_ref[j:j + 1, :] * slab
    ca = jnp.concatenate(acc, axis=0)

    na_ref[0:past - steps] = st_ref[steps:past]
    for s in range(steps):
        na_ref[past - steps + s] = u[s * nb:(s + 1) * nb, :]

    y = _silu(_layernorm_rows(ca, lng_ref[...], lnb_ref[...])) * gz
    ya_ref[...] = _dot(y.astype(BF16), wo_ref[...]).reshape(steps, nb, D_MODEL)


def _mixer_a_sample(h_tm, states, w, cw, cb, lng, lnb, wo, prev, layer, nb):
    steps, n, d = h_tm.shape
    kern = functools.partial(_mixer_a_sample_kernel, steps=steps, nb=nb)
    st = pl.BlockSpec((None, CONV_A - 1, nb, d), lambda i: (layer, 0, i, 0))
    operands = [h_tm, states, w.array, cw, cb, lng, lnb, wo.array]
    slot_specs, slot_args, aliases = _stacked_slot(prev, len(operands), 1)
    return pl.pallas_call(
        kern,
        grid=(n // nb,),
        in_specs=[
            pl.BlockSpec((steps, nb, d), lambda i: (0, i, 0)),
            st,
            w.spec(), _full(cw.shape), _full((1, d)), _full((1, d)), _full((1, d)),
            wo.spec(),
        ] + slot_specs,
        out_specs=[pl.BlockSpec((steps, nb, d), lambda i: (0, i, 0)), st],
        out_shape=[jax.ShapeDtypeStruct((steps, n, d), F32), jax.ShapeDtypeStruct(states.shape, F32)],
        input_output_aliases=aliases,
        compiler_params=_params(("arbitrary",)),
        name="mixer_a_sample",
    )(*operands, *slot_args)


def _mixer_c_kernel(h_ref, w_ref, cw_ref, wo_ref, yc_ref, nc_ref, vbuf, *, tt, nt):
    t = pl.program_id(1)

    @pl.when(t == 0)
    def _():
        vbuf[0:HALO_C, :] = jnp.zeros((HALO_C, D_MODEL), F32)

    h = h_ref[0]
    vbuf[HALO_C:HALO_C + tt, :] = (_dot(h, w_ref[:, D_MODEL:2 * D_MODEL])
                                   * _dot(h, w_ref[:, 2 * D_MODEL:3 * D_MODEL]))
    off = HALO_C - (CONV_C - 1)
    cc = cw_ref[0:1, :] * vbuf[off:off + tt, :]
    for j in range(1, CONV_C):
        cc = cc + cw_ref[j:j + 1, :] * vbuf[off + j:off + j + tt, :]

    @pl.when(t == nt - 1)
    def _():
        nc_ref[0] = vbuf[tt + off:tt + HALO_C, :]

    vbuf[0:HALO_C, :] = vbuf[tt:tt + HALO_C, :]
    y = _dot(h, w_ref[:, :D_MODEL]) * cc * _silu(_dot(h, w_ref[:, 3 * D_MODEL:]))
    yc_ref[0] = _dot(y.astype(BF16), wo_ref[...])


def _mixer_c_prompt(h, w, cw, wo, tt):
    bsz, seq, d = h.shape
    nt = seq // tt
    kern = functools.partial(_mixer_c_kernel, tt=tt, nt=nt)
    return pl.pallas_call(
        kern,
        grid=(bsz, nt),
        in_specs=[
            pl.BlockSpec((1, tt, d), lambda b, t: (b, t, 0)),
            w.spec(), _full(cw.shape), wo.spec(),
        ],
        out_specs=[
            pl.BlockSpec((1, tt, d), lambda b, t: (b, t, 0)),
            pl.BlockSpec((1, CONV_C - 1, d), lambda b, t: (b, 0, 0)),
        ],
        out_shape=[
            jax.ShapeDtypeStruct((bsz, seq, d), F32),
            jax.ShapeDtypeStruct((bsz, CONV_C - 1, d), F32),
        ],
        scratch_shapes=[pltpu.VMEM((HALO_C + tt, d), F32)],
        compiler_params=_params(("arbitrary", "arbitrary")),
        name="mixer_c_prompt",
    )(h, w.array, cw, wo.array)


def _mixer_c_sample_kernel(h_ref, st_ref, w_ref, cw_ref, wo_ref, yc_ref, nc_ref, *, steps, n):
    h = h_ref[...]
    p = _dot(h, w_ref[...])
    v = p[:, D_MODEL:2 * D_MODEL] * p[:, 2 * D_MODEL:3 * D_MODEL]
    past = CONV_C - 1
    slabs = [st_ref[:, i, :] for i in range(past)] + [v[s * n:(s + 1) * n, :] for s in range(steps)]
    rows = []
    for s in range(steps):
        cc = cw_ref[0:1, :] * slabs[s]
        for j in range(1, CONV_C):
            cc = cc + cw_ref[j:j + 1, :] * slabs[s + j]
        rows.append(cc)
    cc = jnp.concatenate(rows, axis=0)
    for i in range(past):
        nc_ref[:, i, :] = slabs[steps + i]
    y = p[:, :D_MODEL] * cc * _silu(p[:, 3 * D_MODEL:])
    yc_ref[...] = _dot(y.astype(BF16), wo_ref[...])


def _mixer_c_sample(h2, state, w, cw, wo, steps):
    rows, d = h2.shape
    n = rows // steps
    kern = functools.partial(_mixer_c_sample_kernel, steps=steps, n=n)
    return pl.pallas_call(
        kern,
        grid=(1,),
        in_specs=[_full((rows, d)), _full(state.shape), w.spec(), _full(cw.shape), wo.spec()],
        out_specs=[_full((rows, d)), _full(state.shape)],
        out_shape=[jax.ShapeDtypeStruct((rows, d), F32), jax.ShapeDtypeStruct(state.shape, F32)],
        compiler_params=_params(("arbitrary",)),
        name="mixer_c_sample",
    )(h2, state, w.array, cw, wo.array)


def _merge_kernel(x_ref, h_ref, ya_ref, yb_ref, yc_ref, w_ref, gb_ref, wo_ref, ng_ref, *out_refs, final):
    h = h_ref[0]
    m = None
    for i, y_ref in enumerate((ya_ref, yb_ref, yc_ref)):
        cs = slice(i * D_MODEL, (i + 1) * D_MODEL)
        term = _sigmoid(_dot(h, w_ref[:, cs]) + gb_ref[:, cs]) * y_ref[0]
        m = term if m is None else m + term
    out = x_ref[0] + _dot(m.astype(BF16), wo_ref[...])
    normed = _rms_rows(out, ng_ref[...])
    if final:
        out_refs[0][0] = normed
    else:
        out_refs[0][0] = out
        out_refs[1][0] = normed.astype(BF16)


def _merge(x, h, ya, yb, yc, w, gate_b, wo, next_g, tt, final):
    bsz, seq, d = x.shape
    tile = pl.BlockSpec((1, tt, d), lambda b, t: (b, t, 0))
    kern = functools.partial(_merge_kernel, final=final)
    out_shape = [jax.ShapeDtypeStruct((bsz, seq, d), F32)]
    if not final:
        out_shape.append(jax.ShapeDtypeStruct((bsz, seq, d), BF16))
    outs = pl.pallas_call(
        kern,
        grid=(bsz, seq // tt),
        in_specs=[tile, tile, tile, tile, tile, w.spec(), _full(gate_b.shape), wo.spec(),
                  _full((1, d))],
        out_specs=[tile] * len(out_shape),
        out_shape=out_shape,
        compiler_params=_params(("arbitrary", "arbitrary")),
        name="merge",
    )(x, h, ya, yb, yc, w.array, gate_b, wo.array, next_g)
    return (outs[0], None) if final else (outs[0], outs[1])


def _rmsnorm_kernel(x_ref, g_ref, h_ref):
    h_ref[0] = _rms_rows(x_ref[0], g_ref[...]).astype(BF16)


def _rmsnorm_bf16(x, g, tt):
    bsz, seq, d = x.shape
    tile = pl.BlockSpec((1, tt, d), lambda b, t: (b, t, 0))
    return pl.pallas_call(
        _rmsnorm_kernel,
        grid=(bsz, seq // tt),
        in_specs=[tile, _full((1, d))],
        out_specs=tile,
        out_shape=jax.ShapeDtypeStruct((bsz, seq, d), BF16),
        compiler_params=_params(("arbitrary", "arbitrary")),
        name="rmsnorm_bf16",
    )(x, g)


def _lower_bound(lbraw_ref, layer):
    depth = lbraw_ref.shape[0]
    rows = [lbraw_ref[i:i + 1, :] for i in range(depth)]
    mx = functools.reduce(jnp.maximum, rows)
    ex = [jnp.exp(r - mx) for r in rows]
    tot = functools.reduce(lambda a, b: a + b, ex)
    csum = []
    run = None
    for e in ex:
        sm = e / tot
        run = sm if run is None else run + sm
        csum.append(run)
    return csum[layer] - csum[0]


def _forget_terms(f_pre, lb):
    e = jnp.exp(-jnp.abs(f_pre))
    r = 1.0 / (1.0 + e)
    er = e * r
    pos = f_pre > 0.0
    log_sig = jnp.minimum(f_pre, 0.0) + jnp.log(r)
    one_m_lb = 1.0 - lb
    f = lb + one_m_lb * jnp.where(pos, r, er)
    logf = jnp.where(lb > 0.0, jnp.log(f), log_sig)
    k = one_m_lb * jnp.where(pos, er, r)
    return logf, k


def _split3(x):
    hi = x.astype(BF16)
    r1 = x - hi.astype(F32)
    mid = r1.astype(BF16)
    lo = (r1 - mid.astype(F32)).astype(BF16)
    return hi, mid, lo


def _hgrn_chunk_consts(c):
    levels = c.bit_length() - 1
    t = np.arange(c)
    col = t[None, :]
    row = t[:, None]
    blocks = [col <= row, col > row]
    masks = []
    for j in range(levels):
        half = 1 << j
        p = ((t >> (j + 1)) << (j + 1)) + half
        tgt = ((t >> j) & 1) == 1
        seg_t = (col >= p[:, None]) & (col <= row)
        seg_s = (col > row) & (col <= p[:, None] - 1)
        blocks.append(np.where(tgt[:, None], seg_t, seg_s))
        same = (row >> (j + 1)) == (col >> (j + 1))
        masks.append(same & tgt[:, None] & (~tgt)[None, :])
    assert (np.sum(np.stack(masks), axis=0) == (col < row)).all()
    seg = np.concatenate(blocks, axis=0).astype(np.float32)
    seg3 = np.concatenate([seg, seg, seg], axis=1)
    pair_masks = np.tile(np.stack(masks).astype(np.float32), (1, 1, 2))
    return jnp.asarray(seg3, BF16), jnp.asarray(pair_masks), levels


def _mixer_b_kernel(h_ref, w_ref, lbraw_ref, hgn_ref, wo_ref, seg_ref, mask_ref, *rest,
                    tt, nt, layer, levels):
    yb_ref, nh_ref, q_s, k_s, lf_s, v_s, gz_s, y_s, dec2_s, att2_s, st_s = rest[-11:]
    t = pl.program_id(1)
    c = HG_CHUNK
    w_bk = HG_HEADS * HG_DK

    @pl.when(t == 0)
    def _():
        st_s[...] = jnp.zeros(st_s.shape, F32)

    h = h_ref[0]
    logf, kk = _forget_terms(_dot(h, w_ref[:, w_bk:2 * w_bk]), _lower_bound(lbraw_ref, layer))
    lf_s[...] = logf
    k_s[...] = kk
    gz_s[...] = _silu(_dot(h, w_ref[:, 3 * w_bk:]))
    q_s[...] = _dot(h, w_ref[:, :w_bk])
    v_s[...] = _dot(h, w_ref[:, 2 * w_bk:3 * w_bk]).astype(BF16)

    def chunk(ci, dec_s, att_s):
        rows = pl.ds(pl.multiple_of(ci * c, c), c)
        hi, mid, lo = _split3(lf_s[rows, :])
        expo = _dot(seg_ref[...], jnp.concatenate([hi, mid, lo], axis=0))
        dec_s[...] = jnp.exp(expo)
        zrow = jnp.zeros((c, HG_DK), BF16)
        zblk = jnp.zeros((HG_DV, HG_DK), BF16)

        def blockdiag(m, z):
            return jnp.concatenate([jnp.concatenate([m[:, :HG_DK], z], axis=1),
                                    jnp.concatenate([z, m[:, HG_DK:]], axis=1)], axis=0)

        for pr in range(HG_HEADS // 2):
            cs2 = slice(2 * pr * HG_DK, 2 * (pr + 1) * HG_DK)
            q2 = q_s[rows, cs2]
            k2 = k_s[rows, cs2]
            att = None
            for j in range(levels):
                gj = dec_s[(2 + j) * c:(3 + j) * c, cs2]
                kj = (k2 * gj).astype(BF16)
                pj = lax.dot_general((q2 * gj).astype(BF16), blockdiag(kj, zrow), NT_DIMS,
                                     preferred_element_type=F32) * mask_ref[j]
                att = pj if att is None else att + pj
            att_s[pr] = att.astype(BF16)

        for pr in range(HG_HEADS // 2):
            cs2 = slice(2 * pr * HG_DK, 2 * (pr + 1) * HG_DK)
            q2 = q_s[rows, cs2]
            k2 = k_s[rows, cs2]
            v2 = v_s[rows, cs2]
            qg = (q2 * dec_s[0:c, cs2]).astype(BF16)
            st_t = jnp.concatenate(
                [jnp.concatenate([st_s[2 * pr].astype(BF16), zblk], axis=1),
                 jnp.concatenate([zblk, st_s[2 * pr + 1].astype(BF16)], axis=1)], axis=0)
            o2 = lax.dot_general(qg, st_t, NT_DIMS, preferred_element_type=F32)
            o2 = o2 + _dot(att_s[pr], blockdiag(v2, zrow))
            qk = q2 * k2
            for i in range(2):
                hd = 2 * pr + i
                cs = slice(hd * HG_DK, (hd + 1) * HG_DK)
                hs = slice(i * HG_DK, (i + 1) * HG_DK)
                o = o2[:, hs] + jnp.sum(qk[:, hs], axis=-1, keepdims=True) * v2[:, hs].astype(F32)
                o = o * lax.rsqrt(jnp.mean(o * o, axis=-1, keepdims=True) + EPS) * hgn_ref[:, cs]
                y_s[rows, cs] = (o * gz_s[rows, cs]).astype(BF16)

        for pr in range(HG_HEADS // 2):
            cs2 = slice(2 * pr * HG_DK, 2 * (pr + 1) * HG_DK)
            khat = (k_s[rows, cs2] * dec_s[c:2 * c, cs2]).astype(BF16)
            upd = lax.dot_general(v_s[rows, cs2], khat, TN_DIMS, preferred_element_type=F32)
            last = dec_s[c - 1:c, cs2]
            for i in range(2):
                hs = slice(i * HG_DK, (i + 1) * HG_DK)
                st_s[2 * pr + i] = st_s[2 * pr + i] * last[:, hs] + upd[hs, hs]

    def chunk_group(cg, carry):
        for e in range(HG_SLOTS):
            chunk(HG_SLOTS * cg + e, dec2_s.at[e], att2_s.at[e])
        return carry

    lax.fori_loop(0, tt // (HG_SLOTS * c), chunk_group, 0)
    yb_ref[0] = _dot(y_s[...], wo_ref[...])

    @pl.when(t == nt - 1)
    def _():
        for hd in range(HG_HEADS):
            nh_ref[0, hd] = st_s[hd].T


def _mixer_b_prompt(h, w, lb_raw, hgn, wo, prev, layer, depth, tt):
    bsz, seq, d = h.shape
    nt = seq // tt
    seg3, masks, levels = _hgrn_chunk_consts(HG_CHUNK)
    w_bk = HG_HEADS * HG_DK
    kern = functools.partial(_mixer_b_kernel, tt=tt, nt=nt, layer=layer, levels=levels)
    operands = [h, w.array, lb_raw, hgn, wo.array, seg3, masks]
    slot_specs, slot_args, aliases = _stacked_slot(prev, len(operands), 1)
    return pl.pallas_call(
        kern,
        grid=(bsz, nt),
        in_specs=[
            pl.BlockSpec((1, tt, d), lambda b, t: (b, t, 0)),
            w.spec(), _full(lb_raw.shape), _full((1, w_bk)), wo.spec(),
            _full(seg3.shape), _full(masks.shape),
        ] + slot_specs,
        out_specs=[
            pl.BlockSpec((1, tt, d), lambda b, t: (b, t, 0)),
            pl.BlockSpec((None, 1, HG_HEADS, HG_DK, HG_DV), lambda b, t: (layer, b, 0, 0, 0)),
        ],
        out_shape=[
            jax.ShapeDtypeStruct((bsz, seq, d), F32),
            jax.ShapeDtypeStruct((depth, bsz, HG_HEADS, HG_DK, HG_DV), F32),
        ],
        input_output_aliases=aliases,
        scratch_shapes=[
            pltpu.VMEM((tt, w_bk), F32),
            pltpu.VMEM((tt, w_bk), F32),
            pltpu.VMEM((tt, w_bk), F32),
            pltpu.VMEM((tt, w_bk), BF16),
            pltpu.VMEM((tt, w_bk), F32),
            pltpu.VMEM((tt, w_bk), BF16),
            pltpu.VMEM((HG_SLOTS, (levels + 2) * HG_CHUNK, w_bk), F32),
            pltpu.VMEM((HG_SLOTS, HG_HEADS // 2, HG_CHUNK, 2 * HG_CHUNK), BF16),
            pltpu.VMEM((HG_HEADS, HG_DV, HG_DK), F32),
        ],
        compiler_params=_params(("arbitrary", "arbitrary")),
        name="mixer_b_prompt",
    )(*operands, *slot_args)


def _mixer_b_sample_proj_kernel(h_ref, w_ref, lbraw_ref,
                                qg_ref, kh_ref, v_ref, eb_ref, oi_ref, gz_ref, *, steps, n, layer):
    w_bk = HG_HEADS * HG_DK
    h = h_ref[...]
    p = _dot(h, w_ref[...])
    logf, kk = _forget_terms(p[:, w_bk:2 * w_bk], _lower_bound(lbraw_ref, layer))
    q = p[:, :w_bk]
    v = p[:, 2 * w_bk:3 * w_bk]
    gz_ref[...] = _silu(p[:, 3 * w_bk:])
    v_ref[...] = v.astype(BF16)

    rs = lambda a, s: a[s * n:(s + 1) * n, :]
    bc = []
    for s in range(steps):
        bc.append(rs(logf, s) if s == 0 else bc[-1] + rs(logf, s))
    eb_ref[...] = jnp.exp(bc[-1])
    ones = jnp.ones((HG_DK, HG_DK), BF16)
    for s in range(steps):
        qg_ref[s * n:(s + 1) * n, :] = (rs(q, s) * jnp.exp(bc[s])).astype(BF16)
        kh_ref[s * n:(s + 1) * n, :] = (rs(kk, s) * jnp.exp(bc[-1] - bc[s])).astype(BF16)
        acc = jnp.zeros((n, w_bk), F32)
        for r in range(s + 1):
            pr = rs(q, s) * rs(kk, r)
            if r < s:
                pr = pr * jnp.exp(bc[s] - bc[r])
            pr = pr.astype(BF16)
            att = jnp.concatenate(
                [_dot(pr[:, hd * HG_DK:(hd + 1) * HG_DK], ones) for hd in range(HG_HEADS)], axis=1)
            acc = acc + att * rs(v, r)
        oi_ref[s * n:(s + 1) * n, :] = acc


def _mixer_b_sample_proj(h2, w, lb_raw, layer, steps):
    rows, d = h2.shape
    n = rows // steps
    w_bk = HG_HEADS * HG_DK
    kern = functools.partial(_mixer_b_sample_proj_kernel, steps=steps, n=n, layer=layer)
    return pl.pallas_call(
        kern,
        grid=(1,),
        in_specs=[_full((rows, d)), w.spec(), _full(lb_raw.shape)],
        out_specs=[_full((rows, w_bk)), _full((rows, w_bk)), _full((rows, w_bk)), _full((n, w_bk)),
                   _full((rows, w_bk)), _full((rows, w_bk))],
        out_shape=[
            jax.ShapeDtypeStruct((rows, w_bk), BF16),
            jax.ShapeDtypeStruct((rows, w_bk), BF16),
            jax.ShapeDtypeStruct((rows, w_bk), BF16),
            jax.ShapeDtypeStruct((n, w_bk), F32),
            jax.ShapeDtypeStruct((rows, w_bk), F32),
            jax.ShapeDtypeStruct((rows, w_bk), F32),
        ],
        compiler_params=_params(("arbitrary",)),
        name="mixer_b_sample_proj",
    )(h2, w.array, lb_raw)


def _mixer_b_sample_state_kernel(st_ref, qg_ref, kh_ref, v_ref, eb_ref, *rest, steps, nb):
    io_ref, ns_ref = rest[-2:]
    pad = SUBLANES - 3 - steps
    ones3 = jnp.ones((3, HG_DV), BF16)
    for i in range(nb):
        for hd in range(HG_HEADS):
            cs = slice(hd * HG_DK, (hd + 1) * HG_DK)
            s = st_ref[i, hd]
            io_ref[:, i, cs] = _dot(qg_ref[:, i, cs], s.astype(BF16))
            hi, mid, lo = _split3(eb_ref[i:i + 1, cs])
            lhs = jnp.concatenate([hi, mid, lo, kh_ref[:, i, cs], jnp.zeros((pad, HG_DK), BF16)], axis=0)
            top = jnp.concatenate([ones3, jnp.zeros((3, HG_DV), BF16)], axis=1)
            mid_rows = jnp.concatenate([jnp.zeros((steps, HG_DV), BF16), v_ref[:, i, cs]], axis=1)
            rhs = jnp.concatenate([top, mid_rows, jnp.zeros((pad, 2 * HG_DV), BF16)], axis=0)
            r = lax.dot_general(lhs, rhs, TN_DIMS, preferred_element_type=F32)
            ns_ref[i, hd] = r[:, :HG_DV] * s + r[:, HG_DV:]


def _mixer_b_sample_state(states, qg, kh, v, eb, prev, layer, nb):
    steps, n, w_bk = qg.shape
    kern = functools.partial(_mixer_b_sample_state_kernel, steps=steps, nb=nb)
    tok = pl.BlockSpec((steps, nb, w_bk), lambda i: (0, i, 0))
    st = pl.BlockSpec((None, nb, HG_HEADS, HG_DK, HG_DV), lambda i: (layer, i, 0, 0, 0))
    operands = [states, qg, kh, v, eb]
    slot_specs, slot_args, aliases = _stacked_slot(prev, len(operands), 1)
    return pl.pallas_call(
        kern,
        grid=(n // nb,),
        in_specs=[st, tok, tok, tok, pl.BlockSpec((nb, w_bk), lambda i: (i, 0))] + slot_specs,
        out_specs=[tok, st],
        out_shape=[jax.ShapeDtypeStruct((steps, n, w_bk), F32), jax.ShapeDtypeStruct(states.shape, F32)],
        input_output_aliases=aliases,
        compiler_params=_params(("arbitrary",)),
        name="mixer_b_sample_state",
    )(*operands, *slot_args)


def _mixer_b_sample_out_kernel(io_ref, oi_ref, gz_ref, hgn_ref, wo_ref, yb_ref):
    o = io_ref[...] + oi_ref[...]
    parts = []
    for hd in range(HG_HEADS):
        oh = o[:, hd * HG_DV:(hd + 1) * HG_DV]
        parts.append(oh * lax.rsqrt(jnp.mean(oh * oh, axis=-1, keepdims=True) + EPS))
    y = jnp.concatenate(parts, axis=1) * hgn_ref[...] * gz_ref[...]
    yb_ref[...] = _dot(y.astype(BF16), wo_ref[...])


def _mixer_b_sample_out(inter, intra, gz, hgn, wo):
    rows, w_bv = inter.shape
    return pl.pallas_call(
        _mixer_b_sample_out_kernel,
        grid=(1,),
        in_specs=[_full((rows, w_bv))] * 3 + [_full((1, w_bv)), wo.spec()],
        out_specs=_full((rows, wo.width)),
        out_shape=jax.ShapeDtypeStruct((rows, wo.width), F32),
        compiler_params=_params(("arbitrary",)),
        name="mixer_b_sample_out",
    )(inter, intra, gz, hgn, wo.array)


def kernel(x_prompt, x_sample, state_conv_a, state_hgrn, state_conv_c, norm_g, w_in, gate_b, conv_a_w,
           conv_a_b, ln_g, ln_b, w_a_out, lb_raw, hg_norm_g, w_b_out, conv_c_w, w_c_out, w_o, final_norm_g):
    depth = w_in.shape[0]
    d = x_prompt.shape[-1]
    n_seq, steps, _ = x_sample.shape
    w_bk = HG_HEADS * HG_DK
    w_bv = HG_HEADS * HG_DV
    col_a = 3 * d
    col_b = col_a + 2 * w_bk + 2 * w_bv
    col_c = col_b + 4 * d
    row = lambda a: a.reshape(1, -1)
    fin_g = row(final_norm_g)

    rows = steps * n_seq
    xp = x_prompt
    xs = jnp.swapaxes(x_sample, 0, 1).reshape(1, rows, d)
    hp = _rmsnorm_bf16(xp, row(norm_g[0]), tt=xp.shape[1])
    hs = _rmsnorm_bf16(xs, row(norm_g[0]), tt=rows)
    conv_a_tm = jnp.swapaxes(state_conv_a, 1, 2)
    pa, pc, sc = [], [], []
    ph = sa = sh = None
    stack_bf16 = lambda w: w.astype(BF16).reshape(depth * w.shape[1], w.shape[2])
    w_in_bf = stack_bf16(w_in)
    out_bf = [stack_bf16(w) for w in (w_a_out, w_b_out, w_c_out, w_o)]
    for l in range(depth):
        cols = lambda a, b: _Cols(w_in_bf, l * d, d, a, b - a)
        w_a, w_b, w_c, w_g = (cols(0, col_a), cols(col_a, col_b), cols(col_b, col_c),
                              cols(col_c, w_in.shape[-1]))
        wao, wbo, wco, woo = (_Cols(w, l * w.shape[0] // depth, w.shape[0] // depth, 0, w.shape[1])
                              for w in out_bf)
        gb = row(gate_b[l])
        cab, lng, lnb, hgn = row(conv_a_b[l]), row(ln_g[l]), row(ln_b[l]), row(hg_norm_g[l])
        final = l == depth - 1
        next_g = fin_g if final else row(norm_g[l + 1])

        ya, na = _mixer_a_prompt(hp, w_a, conv_a_w[l], cab, lng, lnb, wao, tt=PROMPT_TILE)
        yb, ph = _mixer_b_prompt(hp, w_b, lb_raw, hgn, wbo, ph, layer=l, depth=depth, tt=PROMPT_TILE)
        yc, nc = _mixer_c_prompt(hp, w_c, conv_c_w[l], wco, tt=PROMPT_TILE_C)
        xp, hp = _merge(xp, hp, ya, yb, yc, w_g, gb, woo, next_g, tt=PROMPT_TILE, final=final)
        pa.append(na); pc.append(nc)

        hs2 = hs.reshape(rows, d)
        ya, sa = _mixer_a_sample(hs.reshape(steps, n_seq, d), conv_a_tm, w_a, conv_a_w[l], cab, lng, lnb,
                                 wao, sa, layer=l, nb=SAMPLE_CONV_SEQS)
        qg, kh, v, eb, intra, gz = _mixer_b_sample_proj(hs2, w_b, lb_raw, layer=l, steps=steps)
        tm = lambda a: a.reshape(steps, n_seq, w_bk)
        inter, sh = _mixer_b_sample_state(state_hgrn, tm(qg), tm(kh), tm(v), eb, sh, layer=l,
                                          nb=SAMPLE_STATE_SEQS)
        yb = _mixer_b_sample_out(inter.reshape(rows, w_bv), intra, gz, hgn, wbo)
        yc, nc = _mixer_c_sample(hs2, state_conv_c[l], w_c, conv_c_w[l], wco, steps=steps)
        tile = lambda a: a.reshape(1, rows, d)
        xs, hs = _merge(xs, hs, tile(ya), tile(yb), tile(yc), w_g, gb, woo, next_g, tt=rows, final=final)
        sc.append(nc)

    y_sample = jnp.swapaxes(xs.reshape(steps, n_seq, d), 0, 1)
    return (xp, y_sample, jnp.stack(pa), ph, jnp.stack(pc), jnp.swapaxes(sa, 1, 2), sh, jnp.stack(sc))
```

```python
import functools
from typing import NamedTuple

import numpy as np
import jax
import jax.numpy as jnp
from jax import lax
from jax.experimental import pallas as pl
from jax.experimental.pallas import tpu as pltpu

F32 = jnp.float32
BF16 = jnp.bfloat16

EPS = 1e-6
D_MODEL = 1024
HG_HEADS = 8
HG_DK = 128
HG_DV = 128
CONV_A = 31
CONV_C = 3

SUBLANES = 8
LANES = 128
HALO_A = 32
HALO_C = 8
CONV_ROWS = 64
HG_CHUNK = 64
HG_SLOTS = 8
PROMPT_TILE = 512
PROMPT_TILE_C = 1024
SAMPLE_CONV_SEQS = 32
SAMPLE_STATE_SEQS = 8
VMEM_LIMIT = 56 * 1024 * 1024

NT_DIMS = (((1,), (1,)), ((), ()))
TN_DIMS = (((0,), (0,)), ((), ()))


def _sigmoid(x):
    return 1.0 / (1.0 + jnp.exp(-x))


def _silu(x):
    return x * _sigmoid(x)


def _rms_rows(x, g):
    ms = jnp.mean(x * x, axis=-1, keepdims=True)
    return x * lax.rsqrt(ms + EPS) * g


def _layernorm_rows(x, g, b):
    mu = jnp.mean(x, axis=-1, keepdims=True)
    xc = x - mu
    var = jnp.mean(xc * xc, axis=-1, keepdims=True)
    return xc * lax.rsqrt(var + EPS) * g + b


def _dot(a, b):
    return jnp.dot(a, b, preferred_element_type=F32)


def _params(sem):
    return pltpu.CompilerParams(dimension_semantics=sem, vmem_limit_bytes=VMEM_LIMIT,
                                )


def _full(shape):
    n = len(shape)
    return pl.BlockSpec(shape, lambda *_: (0,) * n)


class _Cols(NamedTuple):
    array: jax.Array
    row0: int
    rows: int
    col0: int
    width: int

    def spec(self):
        return pl.BlockSpec((pl.Element(self.rows), pl.Element(self.width)), lambda *_: (self.row0, self.col0))


def _stacked_slot(prev, n_inputs, out_index):
    if prev is None:
        return [], [], {}
    return [pl.BlockSpec(memory_space=pl.ANY)], [prev], {n_inputs: out_index}


def _mixer_a_kernel(h_ref, w_ref, cw_ref, cb_ref, lng_ref, lnb_ref, wo_ref,
                    ya_ref, na_ref, ubuf, cabuf, shbuf, *, tt):
    t = pl.program_id(1)

    @pl.when(t == 0)
    def _():
        ubuf[0:HALO_A, :] = jnp.zeros((HALO_A, D_MODEL), F32)

    h = h_ref[0]
    gate = _sigmoid(_dot(h, w_ref[:, D_MODEL:2 * D_MODEL]))
    ubuf[HALO_A:HALO_A + tt, :] = _dot(h, w_ref[:, :D_MODEL]) * gate
    gz = _silu(_dot(h, w_ref[:, 2 * D_MODEL:]))

    off = HALO_A - (CONV_A - 1)

    nsh = HALO_A + tt - SUBLANES

    def lane_tile(c, carry):
        cs = pl.ds(pl.multiple_of(c * LANES, LANES), LANES)
        for r in range(1, SUBLANES):
            shbuf[r - 1, :, cs] = ubuf[r:r + nsh, cs]
        for rb in range(tt // CONV_ROWS):
            r0 = rb * CONV_ROWS
            acc = jnp.broadcast_to(cb_ref[:, cs], (CONV_ROWS, LANES))
            for j in range(CONV_A):
                base, r = divmod(off + j, SUBLANES)
                lo = r0 + base * SUBLANES
                win = ubuf[lo:lo + CONV_ROWS, cs] if r == 0 else shbuf[r - 1, lo:lo + CONV_ROWS, cs]
                acc = acc + cw_ref[j:j + 1, cs] * win
            cabuf[r0:r0 + CONV_ROWS, cs] = acc
        return carry

    lax.fori_loop(0, D_MODEL // LANES, lane_tile, 0, unroll=True)

    na_ref[0] = ubuf[tt + off:tt + HALO_A, :]

    ubuf[0:HALO_A, :] = ubuf[tt:tt + HALO_A, :]

    y = _silu(_layernorm_rows(cabuf[...], lng_ref[...], lnb_ref[...])) * gz
    ya_ref[0] = _dot(y.astype(BF16), wo_ref[...])


def _mixer_a_prompt(h, w, cw, cb, lng, lnb, wo, tt):
    bsz, seq, d = h.shape
    nt = seq // tt
    kern = functools.partial(_mixer_a_kernel, tt=tt)
    return pl.pallas_call(
        kern,
        grid=(bsz, nt),
        in_specs=[
            pl.BlockSpec((1, tt, d), lambda b, t: (b, t, 0)),
            w.spec(), _full(cw.shape), _full((1, d)), _full((1, d)), _full((1, d)),
            wo.spec(),
        ],
        out_specs=[
            pl.BlockSpec((1, tt, d), lambda b, t: (b, t, 0)),
            pl.BlockSpec((1, CONV_A - 1, d), lambda b, t: (b, 0, 0)),
        ],
        out_shape=[
            jax.ShapeDtypeStruct((bsz, seq, d), F32),
            jax.ShapeDtypeStruct((bsz, CONV_A - 1, d), F32),
        ],
        scratch_shapes=[
            pltpu.VMEM((HALO_A + tt, d), F32),
            pltpu.VMEM((tt, d), F32),
            pltpu.VMEM((SUBLANES - 1, HALO_A + tt - SUBLANES, d), F32),
        ],
        compiler_params=_params(("arbitrary", "arbitrary")),
        name="mixer_a_prompt",
    )(h, w.array, cw, cb, lng, lnb, wo.array)


def _mixer_a_sample_kernel(h_ref, st_ref, w_ref, cw_ref, cb_ref, lng_ref, lnb_ref, wo_ref, *rest,
                           steps, nb):
    ya_ref, na_ref = rest[-2:]
    h = h_ref[...].reshape(steps * nb, D_MODEL)
    p = _dot(h, w_ref[...])
    u = p[:, :D_MODEL] * _sigmoid(p[:, D_MODEL:2 * D_MODEL])
    gz = _silu(p[:, 2 * D_MODEL:])
    past = CONV_A - 1

    acc = [jnp.broadcast_to(cb_ref[...], (nb, D_MODEL)) for _ in range(steps)]
    for i in range(past + steps):
        slab = st_ref[i] if i < past else u[(i - past) * nb:(i - past + 1) * nb, :]
        for s in range(steps):
            j = i - s
            if 0 <= j < CONV_A:
                acc[s] = acc[s] + cw_ref[j:j + 1, :] * slab
    ca = jnp.concatenate(acc, axis=0)

    na_ref[0:past - steps] = st_ref[steps:past]
    for s in range(steps):
        na_ref[past - steps + s] = u[s * nb:(s + 1) * nb, :]

    y = _silu(_layernorm_rows(ca, lng_ref[...], lnb_ref[...])) * gz
    ya_ref[...] = _dot(y.astype(BF16), wo_ref[...]).reshape(steps, nb, D_MODEL)


def _mixer_a_sample(h_tm, states, w, cw, cb, lng, lnb, wo, prev, layer, nb):
    steps, n, d = h_tm.shape
    kern = functools.partial(_mixer_a_sample_kernel, steps=steps, nb=nb)
    st = pl.BlockSpec((None, CONV_A - 1, nb, d), lambda i: (layer, 0, i, 0))
    operands = [h_tm, states, w.array, cw, cb, lng, lnb, wo.array]
    slot_specs, slot_args, aliases = _stacked_slot(prev, len(operands), 1)
    return pl.pallas_call(
        kern,
        grid=(n // nb,),
        in_specs=[
            pl.BlockSpec((steps, nb, d), lambda i: (0, i, 0)),
            st,
            w.spec(), _full(cw.shape), _full((1, d)), _full((1, d)), _full((1, d)),
            wo.spec(),
        ] + slot_specs,
        out_specs=[pl.BlockSpec((steps, nb, d), lambda i: (0, i, 0)), st],
        out_shape=[jax.ShapeDtypeStruct((steps, n, d), F32), jax.ShapeDtypeStruct(states.shape, F32)],
        input_output_aliases=aliases,
        compiler_params=_params(("arbitrary",)),
        name="mixer_a_sample",
    )(*operands, *slot_args)


def _mixer_c_kernel(h_ref, w_ref, cw_ref, wo_ref, yc_ref, nc_ref, vbuf, *, tt):
    t = pl.program_id(1)

    @pl.when(t == 0)
    def _():
        vbuf[0:HALO_C, :] = jnp.zeros((HALO_C, D_MODEL), F32)

    h = h_ref[0]
    vbuf[HALO_C:HALO_C + tt, :] = (_dot(h, w_ref[:, D_MODEL:2 * D_MODEL])
                                   * _dot(h, w_ref[:, 2 * D_MODEL:3 * D_MODEL]))
    off = HALO_C - (CONV_C - 1)
    cc = cw_ref[0:1, :] * vbuf[off:off + tt, :]
    for j in range(1, CONV_C):
        cc = cc + cw_ref[j:j + 1, :] * vbuf[off + j:off + j + tt, :]

    nc_ref[0] = vbuf[tt + off:tt + HALO_C, :]

    vbuf[0:HALO_C, :] = vbuf[tt:tt + HALO_C, :]
    y = _dot(h, w_ref[:, :D_MODEL]) * cc * _silu(_dot(h, w_ref[:, 3 * D_MODEL:]))
    yc_ref[0] = _dot(y.astype(BF16), wo_ref[...])


def _mixer_c_prompt(h, w, cw, wo, tt):
    bsz, seq, d = h.shape
    nt = seq // tt
    kern = functools.partial(_mixer_c_kernel, tt=tt)
    return pl.pallas_call(
        kern,
        grid=(bsz, nt),
        in_specs=[
            pl.BlockSpec((1, tt, d), lambda b, t: (b, t, 0)),
            w.spec(), _full(cw.shape), wo.spec(),
        ],
        out_specs=[
            pl.BlockSpec((1, tt, d), lambda b, t: (b, t, 0)),
            pl.BlockSpec((1, CONV_C - 1, d), lambda b, t: (b, 0, 0)),
        ],
        out_shape=[
            jax.ShapeDtypeStruct((bsz, seq, d), F32),
            jax.ShapeDtypeStruct((bsz, CONV_C - 1, d), F32),
        ],
        scratch_shapes=[pltpu.VMEM((HALO_C + tt, d), F32)],
        compiler_params=_params(("arbitrary", "arbitrary")),
        name="mixer_c_prompt",
    )(h, w.array, cw, wo.array)


def _mixer_c_sample_kernel(h_ref, st_ref, w_ref, cw_ref, wo_ref, yc_ref, nc_ref, *, steps, n):
    h = h_ref[...]
    p = _dot(h, w_ref[...])
    v = p[:, D_MODEL:2 * D_MODEL] * p[:, 2 * D_MODEL:3 * D_MODEL]
    past = CONV_C - 1
    slabs = [st_ref[:, i, :] for i in range(past)] + [v[s * n:(s + 1) * n, :] for s in range(steps)]
    rows = []
    for s in range(steps):
        cc = cw_ref[0:1, :] * slabs[s]
        for j in range(1, CONV_C):
            cc = cc + cw_ref[j:j + 1, :] * slabs[s + j]
        rows.append(cc)
    cc = jnp.concatenate(rows, axis=0)
    for i in range(past):
        nc_ref[:, i, :] = slabs[steps + i]
    y = p[:, :D_MODEL] * cc * _silu(p[:, 3 * D_MODEL:])
    yc_ref[...] = _dot(y.astype(BF16), wo_ref[...])


def _mixer_c_sample(h2, state, w, cw, wo, steps):
    rows, d = h2.shape
    n = rows // steps
    kern = functools.partial(_mixer_c_sample_kernel, steps=steps, n=n)
    return pl.pallas_call(
        kern,
        grid=(1,),
        in_specs=[_full((rows, d)), _full(state.shape), w.spec(), _full(cw.shape), wo.spec()],
        out_specs=[_full((rows, d)), _full(state.shape)],
        out_shape=[jax.ShapeDtypeStruct((rows, d), F32), jax.ShapeDtypeStruct(state.shape, F32)],
        compiler_params=_params(("arbitrary",)),
        name="mixer_c_sample",
    )(h2, state, w.array, cw, wo.array)


def _merge_kernel(x_ref, h_ref, ya_ref, yb_ref, yc_ref, w_ref, gb_ref, wo_ref, ng_ref, *out_refs, final):
    h = h_ref[0]
    m = None
    for i, y_ref in enumerate((ya_ref, yb_ref, yc_ref)):
        cs = slice(i * D_MODEL, (i + 1) * D_MODEL)
        term = _sigmoid(_dot(h, w_ref[:, cs]) + gb_ref[:, cs]) * y_ref[0]
        m = term if m is None else m + term
    out = x_ref[0] + _dot(m.astype(BF16), wo_ref[...])
    normed = _rms_rows(out, ng_ref[...])
    if final:
        out_refs[0][0] = normed
    else:
        out_refs[0][0] = out
        out_refs[1][0] = normed.astype(BF16)


def _merge(x, h, ya, yb, yc, w, gate_b, wo, next_g, tt, final):
    bsz, seq, d = x.shape
    tile = pl.BlockSpec((1, tt, d), lambda b, t: (b, t, 0))
    kern = functools.partial(_merge_kernel, final=final)
    out_shape = [jax.ShapeDtypeStruct((bsz, seq, d), F32)]
    if not final:
        out_shape.append(jax.ShapeDtypeStruct((bsz, seq, d), BF16))
    outs = pl.pallas_call(
        kern,
        grid=(bsz, seq // tt),
        in_specs=[tile, tile, tile, tile, tile, w.spec(), _full(gate_b.shape), wo.spec(),
                  _full((1, d))],
        out_specs=[tile] * len(out_shape),
        out_shape=out_shape,
        compiler_params=_params(("arbitrary", "arbitrary")),
        name="merge",
    )(x, h, ya, yb, yc, w.array, gate_b, wo.array, next_g)
    return (outs[0], None) if final else (outs[0], outs[1])


def _rmsnorm_kernel(x_ref, g_ref, h_ref):
    h_ref[0] = _rms_rows(x_ref[0], g_ref[...]).astype(BF16)


def _rmsnorm_bf16(x, g, tt):
    bsz, seq, d = x.shape
    tile = pl.BlockSpec((1, tt, d), lambda b, t: (b, t, 0))
    return pl.pallas_call(
        _rmsnorm_kernel,
        grid=(bsz, seq // tt),
        in_specs=[tile, _full((1, d))],
        out_specs=tile,
        out_shape=jax.ShapeDtypeStruct((bsz, seq, d), BF16),
        compiler_params=_params(("arbitrary", "arbitrary")),
        name="rmsnorm_bf16",
    )(x, g)


def _lower_bound(lbraw_ref, layer):
    depth = lbraw_ref.shape[0]
    rows = [lbraw_ref[i:i + 1, :] for i in range(depth)]
    mx = functools.reduce(jnp.maximum, rows)
    ex = [jnp.exp(r - mx) for r in rows]
    tot = functools.reduce(lambda a, b: a + b, ex)
    csum = []
    run = None
    for e in ex:
        sm = e / tot
        run = sm if run is None else run + sm
        csum.append(run)
    return csum[layer] - csum[0]


def _forget_terms(f_pre, lb):
    e = jnp.exp(-jnp.abs(f_pre))
    r = 1.0 / (1.0 + e)
    er = e * r
    pos = f_pre > 0.0
    log_sig = jnp.minimum(f_pre, 0.0) + jnp.log(r)
    one_m_lb = 1.0 - lb
    f = lb + one_m_lb * jnp.where(pos, r, er)
    logf = jnp.where(lb > 0.0, jnp.log(f), log_sig)
    k = one_m_lb * jnp.where(pos, er, r)
    return logf, k


def _split3(x):
    hi = x.astype(BF16)
    r1 = x - hi.astype(F32)
    mid = r1.astype(BF16)
    lo = (r1 - mid.astype(F32)).astype(BF16)
    return hi, mid, lo


def _hgrn_chunk_consts(c):
    levels = c.bit_length() - 1
    t = np.arange(c)
    col = t[None, :]
    row = t[:, None]
    blocks = [col <= row, col > row]
    masks = []
    for j in range(levels):
        half = 1 << j
        p = ((t >> (j + 1)) << (j + 1)) + half
        tgt = ((t >> j) & 1) == 1
        seg_t = (col >= p[:, None]) & (col <= row)
        seg_s = (col > row) & (col <= p[:, None] - 1)
        blocks.append(np.where(tgt[:, None], seg_t, seg_s))
        same = (row >> (j + 1)) == (col >> (j + 1))
        masks.append(same & tgt[:, None] & (~tgt)[None, :])
    assert (np.sum(np.stack(masks), axis=0) == (col < row)).all()
    seg = np.concatenate(blocks, axis=0).astype(np.float32)
    seg3 = np.concatenate([seg, seg, seg], axis=1)
    pair_masks = np.tile(np.stack(masks).astype(np.float32), (1, 1, 2))
    return jnp.asarray(seg3, BF16), jnp.asarray(pair_masks), levels


def _mixer_b_kernel(h_ref, w_ref, lbraw_ref, hgn_ref, wo_ref, seg_ref, mask_ref, *rest,
                    tt, nt, layer, levels):
    yb_ref, nh_ref, q_s, k_s, lf_s, v_s, gz_s, y_s, dec2_s, att2_s, st_s = rest[-11:]
    t = pl.program_id(1)
    c = HG_CHUNK
    w_bk = HG_HEADS * HG_DK

    @pl.when(t == 0)
    def _():
        st_s[...] = jnp.zeros(st_s.shape, F32)

    h = h_ref[0]
    logf, kk = _forget_terms(_dot(h, w_ref[:, w_bk:2 * w_bk]), _lower_bound(lbraw_ref, layer))
    lf_s[...] = logf
    k_s[...] = kk
    gz_s[...] = _silu(_dot(h, w_ref[:, 3 * w_bk:]))
    q_s[...] = _dot(h, w_ref[:, :w_bk])
    v_s[...] = _dot(h, w_ref[:, 2 * w_bk:3 * w_bk]).astype(BF16)

    def chunk(ci, dec_s, att_s):
        rows = pl.ds(pl.multiple_of(ci * c, c), c)
        hi, mid, lo = _split3(lf_s[rows, :])
        expo = _dot(seg_ref[...], jnp.concatenate([hi, mid, lo], axis=0))
        dec_s[...] = jnp.exp(expo)
        zrow = jnp.zeros((c, HG_DK), BF16)
        zblk = jnp.zeros((HG_DV, HG_DK), BF16)

        def blockdiag(m, z):
            return jnp.concatenate([jnp.concatenate([m[:, :HG_DK], z], axis=1),
                                    jnp.concatenate([z, m[:, HG_DK:]], axis=1)], axis=0)

        for pr in range(HG_HEADS // 2):
            cs2 = slice(2 * pr * HG_DK, 2 * (pr + 1) * HG_DK)
            q2 = q_s[rows, cs2]
            k2 = k_s[rows, cs2]
            att = None
            for j in range(levels):
                gj = dec_s[(2 + j) * c:(3 + j) * c, cs2]
                kj = (k2 * gj).astype(BF16)
                pj = lax.dot_general((q2 * gj).astype(BF16), blockdiag(kj, zrow), NT_DIMS,
                                     preferred_element_type=F32) * mask_ref[j]
                att = pj if att is None else att + pj
            att_s[pr] = att.astype(BF16)

        for pr in range(HG_HEADS // 2):
            cs2 = slice(2 * pr * HG_DK, 2 * (pr + 1) * HG_DK)
            q2 = q_s[rows, cs2]
            k2 = k_s[rows, cs2]
            v2 = v_s[rows, cs2]
            qg = (q2 * dec_s[0:c, cs2]).astype(BF16)
            st_t = jnp.concatenate(
                [jnp.concatenate([st_s[2 * pr].astype(BF16), zblk], axis=1),
                 jnp.concatenate([zblk, st_s[2 * pr + 1].astype(BF16)], axis=1)], axis=0)
            o2 = lax.dot_general(qg, st_t, NT_DIMS, preferred_element_type=F32)
            o2 = o2 + _dot(att_s[pr], blockdiag(v2, zrow))
            qk = q2 * k2
            for i in range(2):
                hd = 2 * pr + i
                cs = slice(hd * HG_DK, (hd + 1) * HG_DK)
                hs = slice(i * HG_DK, (i + 1) * HG_DK)
                o = o2[:, hs] + jnp.sum(qk[:, hs], axis=-1, keepdims=True) * v2[:, hs].astype(F32)
                o = o * lax.rsqrt(jnp.mean(o * o, axis=-1, keepdims=True) + EPS) * hgn_ref[:, cs]
                y_s[rows, cs] = (o * gz_s[rows, cs]).astype(BF16)

        for pr in range(HG_HEADS // 2):
            cs2 = slice(2 * pr * HG_DK, 2 * (pr + 1) * HG_DK)
            khat = (k_s[rows, cs2] * dec_s[c:2 * c, cs2]).astype(BF16)
            upd = lax.dot_general(v_s[rows, cs2], khat, TN_DIMS, preferred_element_type=F32)
            last = dec_s[c - 1:c, cs2]
            for i in range(2):
                hs = slice(i * HG_DK, (i + 1) * HG_DK)
                st_s[2 * pr + i] = st_s[2 * pr + i] * last[:, hs] + upd[hs, hs]

    def chunk_group(cg, carry):
        for e in range(HG_SLOTS):
            chunk(HG_SLOTS * cg + e, dec2_s.at[e], att2_s.at[e])
        return carry

    lax.fori_loop(0, tt // (HG_SLOTS * c), chunk_group, 0)
    yb_ref[0] = _dot(y_s[...], wo_ref[...])

    @pl.when(t == nt - 1)
    def _():
        for hd in range(HG_HEADS):
            nh_ref[0, hd] = st_s[hd].T


def _mixer_b_prompt(h, w, lb_raw, hgn, wo, prev, layer, depth, tt):
    bsz, seq, d = h.shape
    nt = seq // tt
    seg3, masks, levels = _hgrn_chunk_consts(HG_CHUNK)
    w_bk = HG_HEADS * HG_DK
    kern = functools.partial(_mixer_b_kernel, tt=tt, nt=nt, layer=layer, levels=levels)
    operands = [h, w.array, lb_raw, hgn, wo.array, seg3, masks]
    slot_specs, slot_args, aliases = _stacked_slot(prev, len(operands), 1)
    return pl.pallas_call(
        kern,
        grid=(bsz, nt),
        in_specs=[
            pl.BlockSpec((1, tt, d), lambda b, t: (b, t, 0)),
            w.spec(), _full(lb_raw.shape), _full((1, w_bk)), wo.spec(),
            _full(seg3.shape), _full(masks.shape),
        ] + slot_specs,
        out_specs=[
            pl.BlockSpec((1, tt, d), lambda b, t: (b, t, 0)),
            pl.BlockSpec((None, 1, HG_HEADS, HG_DK, HG_DV), lambda b, t: (layer, b, 0, 0, 0)),
        ],
        out_shape=[
            jax.ShapeDtypeStruct((bsz, seq, d), F32),
            jax.ShapeDtypeStruct((depth, bsz, HG_HEADS, HG_DK, HG_DV), F32),
        ],
        input_output_aliases=aliases,
        scratch_shapes=[
            pltpu.VMEM((tt, w_bk), F32),
            pltpu.VMEM((tt, w_bk), F32),
            pltpu.VMEM((tt, w_bk), F32),
            pltpu.VMEM((tt, w_bk), BF16),
            pltpu.VMEM((tt, w_bk), F32),
            pltpu.VMEM((tt, w_bk), BF16),
            pltpu.VMEM((HG_SLOTS, (levels + 2) * HG_CHUNK, w_bk), F32),
            pltpu.VMEM((HG_SLOTS, HG_HEADS // 2, HG_CHUNK, 2 * HG_CHUNK), BF16),
            pltpu.VMEM((HG_HEADS, HG_DV, HG_DK), F32),
        ],
        compiler_params=_params(("arbitrary", "arbitrary")),
        name="mixer_b_prompt",
    )(*operands, *slot_args)


def _mixer_b_sample_proj_kernel(h_ref, w_ref, lbraw_ref,
                                qg_ref, kh_ref, v_ref, eb_ref, oi_ref, gz_ref, *, steps, n, layer):
    w_bk = HG_HEADS * HG_DK
    h = h_ref[...]
    p = _dot(h, w_ref[...])
    logf, kk = _forget_terms(p[:, w_bk:2 * w_bk], _lower_bound(lbraw_ref, layer))
    q = p[:, :w_bk]
    v = p[:, 2 * w_bk:3 * w_bk]
    gz_ref[...] = _silu(p[:, 3 * w_bk:])
    v_ref[...] = v.astype(BF16)

    rs = lambda a, s: a[s * n:(s + 1) * n, :]
    bc = []
    for s in range(steps):
        bc.append(rs(logf, s) if s == 0 else bc[-1] + rs(logf, s))
    eb_ref[...] = jnp.exp(bc[-1])
    ones = jnp.ones((HG_DK, HG_DK), BF16)
    for s in range(steps):
        qg_ref[s * n:(s + 1) * n, :] = (rs(q, s) * jnp.exp(bc[s])).astype(BF16)
        kh_ref[s * n:(s + 1) * n, :] = (rs(kk, s) * jnp.exp(bc[-1] - bc[s])).astype(BF16)
        acc = jnp.zeros((n, w_bk), F32)
        for r in range(s + 1):
            pr = rs(q, s) * rs(kk, r)
            if r < s:
                pr = pr * jnp.exp(bc[s] - bc[r])
            pr = pr.astype(BF16)
            att = jnp.concatenate(
                [_dot(pr[:, hd * HG_DK:(hd + 1) * HG_DK], ones) for hd in range(HG_HEADS)], axis=1)
            acc = acc + att * rs(v, r)
        oi_ref[s * n:(s + 1) * n, :] = acc


def _mixer_b_sample_proj(h2, w, lb_raw, layer, steps):
    rows, d = h2.shape
    n = rows // steps
    w_bk = HG_HEADS * HG_DK
    kern = functools.partial(_mixer_b_sample_proj_kernel, steps=steps, n=n, layer=layer)
    return pl.pallas_call(
        kern,
        grid=(1,),
        in_specs=[_full((rows, d)), w.spec(), _full(lb_raw.shape)],
        out_specs=[_full((rows, w_bk)), _full((rows, w_bk)), _full((rows, w_bk)), _full((n, w_bk)),
                   _full((rows, w_bk)), _full((rows, w_bk))],
        out_shape=[
            jax.ShapeDtypeStruct((rows, w_bk), BF16),
            jax.ShapeDtypeStruct((rows, w_bk), BF16),
            jax.ShapeDtypeStruct((rows, w_bk), BF16),
            jax.ShapeDtypeStruct((n, w_bk), F32),
            jax.ShapeDtypeStruct((rows, w_bk), F32),
            jax.ShapeDtypeStruct((rows, w_bk), F32),
        ],
        compiler_params=_params(("arbitrary",)),
        name="mixer_b_sample_proj",
    )(h2, w.array, lb_raw)


def _mixer_b_sample_state_kernel(st_ref, qg_ref, kh_ref, v_ref, eb_ref, *rest, steps, nb):
    io_ref, ns_ref = rest[-2:]
    pad = SUBLANES - 3 - steps
    ones3 = jnp.ones((3, HG_DV), BF16)
    for i in range(nb):
        for hd in range(HG_HEADS):
            cs = slice(hd * HG_DK, (hd + 1) * HG_DK)
            s = st_ref[i, hd]
            io_ref[:, i, cs] = _dot(qg_ref[:, i, cs], s.astype(BF16))
            hi, mid, lo = _split3(eb_ref[i:i + 1, cs])
            lhs = jnp.concatenate([hi, mid, lo, kh_ref[:, i, cs], jnp.zeros((pad, HG_DK), BF16)], axis=0)
            top = jnp.concatenate([ones3, jnp.zeros((3, HG_DV), BF16)], axis=1)
            mid_rows = jnp.concatenate([jnp.zeros((steps, HG_DV), BF16), v_ref[:, i, cs]], axis=1)
            rhs = jnp.concatenate([top, mid_rows, jnp.zeros((pad, 2 * HG_DV), BF16)], axis=0)
            r = lax.dot_general(lhs, rhs, TN_DIMS, preferred_element_type=F32)
            ns_ref[i, hd] = r[:, :HG_DV] * s + r[:, HG_DV:]


def _mixer_b_sample_state(states, qg, kh, v, eb, prev, layer, nb):
    steps, n, w_bk = qg.shape
    kern = functools.partial(_mixer_b_sample_state_kernel, steps=steps, nb=nb)
    tok = pl.BlockSpec((steps, nb, w_bk), lambda i: (0, i, 0))
    st = pl.BlockSpec((None, nb, HG_HEADS, HG_DK, HG_DV), lambda i: (layer, i, 0, 0, 0))
    operands = [states, qg, kh, v, eb]
    slot_specs, slot_args, aliases = _stacked_slot(prev, len(operands), 1)
    return pl.pallas_call(
        kern,
        grid=(n // nb,),
        in_specs=[st, tok, tok, tok, pl.BlockSpec((nb, w_bk), lambda i: (i, 0))] + slot_specs,
        out_specs=[tok, st],
        out_shape=[jax.ShapeDtypeStruct((steps, n, w_bk), F32), jax.ShapeDtypeStruct(states.shape, F32)],
        input_output_aliases=aliases,
        compiler_params=_params(("arbitrary",)),
        name="mixer_b_sample_state",
    )(*operands, *slot_args)


def _mixer_b_sample_out_kernel(io_ref, oi_ref, gz_ref, hgn_ref, wo_ref, yb_ref):
    o = io_ref[...] + oi_ref[...]
    parts = []
    for hd in range(HG_HEADS):
        oh = o[:, hd * HG_DV:(hd + 1) * HG_DV]
        parts.append(oh * lax.rsqrt(jnp.mean(oh * oh, axis=-1, keepdims=True) + EPS))
    y = jnp.concatenate(parts, axis=1) * hgn_ref[...] * gz_ref[...]
    yb_ref[...] = _dot(y.astype(BF16), wo_ref[...])


def _mixer_b_sample_out(inter, intra, gz, hgn, wo):
    rows, w_bv = inter.shape
    return pl.pallas_call(
        _mixer_b_sample_out_kernel,
        grid=(1,),
        in_specs=[_full((rows, w_bv))] * 3 + [_full((1, w_bv)), wo.spec()],
        out_specs=_full((rows, wo.width)),
        out_shape=jax.ShapeDtypeStruct((rows, wo.width), F32),
        compiler_params=_params(("arbitrary",)),
        name="mixer_b_sample_out",
    )(inter, intra, gz, hgn, wo.array)


def kernel(x_prompt, x_sample, state_conv_a, state_hgrn, state_conv_c, norm_g, w_in, gate_b, conv_a_w,
           conv_a_b, ln_g, ln_b, w_a_out, lb_raw, hg_norm_g, w_b_out, conv_c_w, w_c_out, w_o, final_norm_g):
    depth = w_in.shape[0]
    d = x_prompt.shape[-1]
    n_seq, steps, _ = x_sample.shape
    w_bk = HG_HEADS * HG_DK
    w_bv = HG_HEADS * HG_DV
    col_a = 3 * d
    col_b = col_a + 2 * w_bk + 2 * w_bv
    col_c = col_b + 4 * d
    row = lambda a: a.reshape(1, -1)
    fin_g = row(final_norm_g)

    rows = steps * n_seq
    xp = x_prompt
    xs = jnp.swapaxes(x_sample, 0, 1).reshape(1, rows, d)
    hp = _rmsnorm_bf16(xp, row(norm_g[0]), tt=xp.shape[1])
    hs = _rmsnorm_bf16(xs, row(norm_g[0]), tt=rows)
    conv_a_tm = jnp.swapaxes(state_conv_a, 1, 2)
    pa, pc, sc = [], [], []
    ph = sa = sh = None
    stack_bf16 = lambda w: w.astype(BF16).reshape(depth * w.shape[1], w.shape[2])
    w_in_bf = stack_bf16(w_in)
    out_bf = [stack_bf16(w) for w in (w_a_out, w_b_out, w_c_out, w_o)]
    for l in range(depth):
        cols = lambda a, b: _Cols(w_in_bf, l * d, d, a, b - a)
        w_a, w_b, w_c, w_g = (cols(0, col_a), cols(col_a, col_b), cols(col_b, col_c),
                              cols(col_c, w_in.shape[-1]))
        wao, wbo, wco, woo = (_Cols(w, l * w.shape[0] // depth, w.shape[0] // depth, 0, w.shape[1])
                              for w in out_bf)
        gb = row(gate_b[l])
        cab, lng, lnb, hgn = row(conv_a_b[l]), row(ln_g[l]), row(ln_b[l]), row(hg_norm_g[l])
        final = l == depth - 1
        next_g = fin_g if final else row(norm_g[l + 1])

        ya, na = _mixer_a_prompt(hp, w_a, conv_a_w[l], cab, lng, lnb, wao, tt=PROMPT_TILE)
        yb, ph = _mixer_b_prompt(hp, w_b, lb_raw, hgn, wbo, ph, layer=l, depth=depth, tt=PROMPT_TILE)
        yc, nc = _mixer_c_prompt(hp, w_c, conv_c_w[l], wco, tt=PROMPT_TILE_C)
        xp, hp = _merge(xp, hp, ya, yb, yc, w_g, gb, woo, next_g, tt=PROMPT_TILE, final=final)
        pa.append(na); pc.append(nc)

        hs2 = hs.reshape(rows, d)
        ya, sa = _mixer_a_sample(hs.reshape(steps, n_seq, d), conv_a_tm, w_a, conv_a_w[l], cab, lng, lnb,
                                 wao, sa, layer=l, nb=SAMPLE_CONV_SEQS)
        qg, kh, v, eb, intra, gz = _mixer_b_sample_proj(hs2, w_b, lb_raw, layer=l, steps=steps)
        tm = lambda a: a.reshape(steps, n_seq, w_bk)
        inter, sh = _mixer_b_sample_state(state_hgrn, tm(qg), tm(kh), tm(v), eb, sh, layer=l,
                                          nb=SAMPLE_STATE_SEQS)
        yb = _mixer_b_sample_out(inter.reshape(rows, w_bv), intra, gz, hgn, wbo)
        yc, nc = _mixer_c_sample(hs2, state_conv_c[l], w_c, conv_c_w[l], wco, steps=steps)
        tile = lambda a: a.reshape(1, rows, d)
        xs, hs = _merge(xs, hs, tile(ya), tile(yb), tile(yc), w_g, gb, woo, next_g, tt=rows, final=final)
        sc.append(nc)

    y_sample = jnp.swapaxes(xs.reshape(steps, n_seq, d), 0, 1)
    return (xp, y_sample, jnp.stack(pa), ph, jnp.stack(pc), jnp.swapaxes(sa, 1, 2), sh, jnp.stack(sc))
```

```python
import functools
from typing import NamedTuple

import numpy as np
import jax
import jax.numpy as jnp
from jax import lax
from jax.experimental import pallas as pl
from jax.experimental.pallas import tpu as pltpu

F32 = jnp.float32
BF16 = jnp.bfloat16

EPS = 1e-6
D_MODEL = 1024
HG_HEADS = 8
HG_DK = 128
HG_DV = 128
CONV_A = 31
CONV_C = 3

SUBLANES = 8
LANES = 128
HALO_A = 32
HALO_C = 8
CONV_ROWS = 64
HG_CHUNK = 64
HG_SLOTS = 8
PROMPT_TILE = 512
PROMPT_TILE_C = 1024
SAMPLE_CONV_SEQS = 32
SAMPLE_STATE_SEQS = 8
VMEM_LIMIT = 56 * 1024 * 1024

NT_DIMS = (((1,), (1,)), ((), ()))
TN_DIMS = (((0,), (0,)), ((), ()))


def _sigmoid(x):
    return 1.0 / (1.0 + jnp.exp(-x))


def _silu(x):
    return x * _sigmoid(x)


def _rms_rows(x, g):
    ms = jnp.mean(x * x, axis=-1, keepdims=True)
    return x * lax.rsqrt(ms + EPS) * g


def _layernorm_rows(x, g, b):
    mu = jnp.mean(x, axis=-1, keepdims=True)
    xc = x - mu
    var = jnp.mean(xc * xc, axis=-1, keepdims=True)
    return xc * lax.rsqrt(var + EPS) * g + b


def _dot(a, b):
    return jnp.dot(a, b, preferred_element_type=F32)


def _params(sem):
    return pltpu.CompilerParams(dimension_semantics=sem, vmem_limit_bytes=VMEM_LIMIT,
                                )


def _full(shape):
    n = len(shape)
    return pl.BlockSpec(shape, lambda *_: (0,) * n)


class _Cols(NamedTuple):
    array: jax.Array
    row0: int
    rows: int
    col0: int
    width: int

    def spec(self):
        return pl.BlockSpec((pl.Element(self.rows), pl.Element(self.width)), lambda *_: (self.row0, self.col0))


def _stacked_slot(prev, n_inputs, out_index):
    if prev is None:
        return [], [], {}
    return [pl.BlockSpec(memory_space=pl.ANY)], [prev], {n_inputs: out_index}


def _mixer_a_kernel(h_ref, w_ref, cw_ref, cb_ref, lng_ref, lnb_ref, wo_ref,
                    ya_ref, na_ref, ubuf, cabuf, shbuf, *, tt):
    t = pl.program_id(1)

    @pl.when(t == 0)
    def _():
        ubuf[0:HALO_A, :] = jnp.zeros((HALO_A, D_MODEL), F32)

    h = h_ref[0]
    gate = _sigmoid(_dot(h, w_ref[:, D_MODEL:2 * D_MODEL]))
    ubuf[HALO_A:HALO_A + tt, :] = _dot(h, w_ref[:, :D_MODEL]) * gate
    gz = _silu(_dot(h, w_ref[:, 2 * D_MODEL:]))

    off = HALO_A - (CONV_A - 1)

    nsh = HALO_A + tt - SUBLANES

    def lane_tile(c, carry):
        cs = pl.ds(pl.multiple_of(c * LANES, LANES), LANES)
        for r in range(1, SUBLANES):
            shbuf[r - 1, :, cs] = ubuf[r:r + nsh, cs]
        for rb in range(tt // CONV_ROWS):
            r0 = rb * CONV_ROWS
            acc = jnp.broadcast_to(cb_ref[:, cs], (CONV_ROWS, LANES))
            for j in range(CONV_A):
                base, r = divmod(off + j, SUBLANES)
                lo = r0 + base * SUBLANES
                win = ubuf[lo:lo + CONV_ROWS, cs] if r == 0 else shbuf[r - 1, lo:lo + CONV_ROWS, cs]
                acc = acc + cw_ref[j:j + 1, cs] * win
            cabuf[r0:r0 + CONV_ROWS, cs] = acc
        return carry

    lax.fori_loop(0, D_MODEL // LANES, lane_tile, 0, unroll=True)

    na_ref[0] = ubuf[tt + off:tt + HALO_A, :]

    ubuf[0:HALO_A, :] = ubuf[tt:tt + HALO_A, :]

    y = _silu(_layernorm_rows(cabuf[...], lng_ref[...], lnb_ref[...])) * gz
    ya_ref[0] = _dot(y.astype(BF16), wo_ref[...])


def _mixer_a_prompt(h, w, cw, cb, lng, lnb, wo, tt):
    bsz, seq, d = h.shape
    nt = seq // tt
    kern = functools.partial(_mixer_a_kernel, tt=tt)
    return pl.pallas_call(
        kern,
        grid=(bsz, nt),
        in_specs=[
            pl.BlockSpec((1, tt, d), lambda b, t: (b, t, 0)),
            w.spec(), _full(cw.shape), _full((1, d)), _full((1, d)), _full((1, d)),
            wo.spec(),
        ],
        out_specs=[
            pl.BlockSpec((1, tt, d), lambda b, t: (b, t, 0)),
            pl.BlockSpec((1, CONV_A - 1, d), lambda b, t: (b, 0, 0)),
        ],
        out_shape=[
            jax.ShapeDtypeStruct((bsz, seq, d), F32),
            jax.ShapeDtypeStruct((bsz, CONV_A - 1, d), F32),
        ],
        scratch_shapes=[
            pltpu.VMEM((HALO_A + tt, d), F32),
            pltpu.VMEM((tt, d), F32),
            pltpu.VMEM((SUBLANES - 1, HALO_A + tt - SUBLANES, d), F32),
        ],
        compiler_params=_params(("arbitrary", "arbitrary")),
        name="mixer_a_prompt",
    )(h, w.array, cw, cb, lng, lnb, wo.array)


def _mixer_a_sample_kernel(h_ref, st_ref, w_ref, cw_ref, cb_ref, lng_ref, lnb_ref, wo_ref, *rest,
                           steps, nb):
    ya_ref, na_ref = rest[-2:]
    h = h_ref[...].reshape(steps * nb, D_MODEL)
    p = _dot(h, w_ref[...])
    u = p[:, :D_MODEL] * _sigmoid(p[:, D_MODEL:2 * D_MODEL])
    gz = _silu(p[:, 2 * D_MODEL:])
    past = CONV_A - 1

    acc = [jnp.broadcast_to(cb_ref[...], (nb, D_MODEL)) for _ in range(steps)]
    for i in range(past + steps):
        slab = st_ref[i] if i < past else u[(i - past) * nb:(i - past + 1) * nb, :]
        for s in range(steps):
            j = i - s
            if 0 <= j < CONV_A:
                acc[s] = acc[s] + cw_ref[j:j + 1, :] * slab
    ca = jnp.concatenate(acc, axis=0)

    na_ref[0:past - steps] = st_ref[steps:past]
    for s in range(steps):
        na_ref[past - steps + s] = u[s * nb:(s + 1) * nb, :]

    y = _silu(_layernorm_rows(ca, lng_ref[...], lnb_ref[...])) * gz
    ya_ref[...] = _dot(y.astype(BF16), wo_ref[...]).reshape(steps, nb, D_MODEL)


def _mixer_a_sample(h_tm, states, w, cw, cb, lng, lnb, wo, prev, layer, nb):
    steps, n, d = h_tm.shape
    kern = functools.partial(_mixer_a_sample_kernel, steps=steps, nb=nb)
    st = pl.BlockSpec((None, CONV_A - 1, nb, d), lambda i: (layer, 0, i, 0))
    operands = [h_tm, states, w.array, cw, cb, lng, lnb, wo.array]
    slot_specs, slot_args, aliases = _stacked_slot(prev, len(operands), 1)
    return pl.pallas_call(
        kern,
        grid=(n // nb,),
        in_specs=[
            pl.BlockSpec((steps, nb, d), lambda i: (0, i, 0)),
            st,
            w.spec(), _full(cw.shape), _full((1, d)), _full((1, d)), _full((1, d)),
            wo.spec(),
        ] + slot_specs,
        out_specs=[pl.BlockSpec((steps, nb, d), lambda i: (0, i, 0)), st],
        out_shape=[jax.ShapeDtypeStruct((steps, n, d), F32), jax.ShapeDtypeStruct(states.shape, F32)],
        input_output_aliases=aliases,
        compiler_params=_params(("arbitrary",)),
        name="mixer_a_sample",
    )(*operands, *slot_args)


def _mixer_c_kernel(h_ref, w_ref, cw_ref, wo_ref, yc_ref, nc_ref, vbuf, *, tt):
    t = pl.program_id(1)

    @pl.when(t == 0)
    def _():
        vbuf[0:HALO_C, :] = jnp.zeros((HALO_C, D_MODEL), F32)

    h = h_ref[0]
    vbuf[HALO_C:HALO_C + tt, :] = (_dot(h, w_ref[:, D_MODEL:2 * D_MODEL])
                                   * _dot(h, w_ref[:, 2 * D_MODEL:3 * D_MODEL]))
    off = HALO_C - (CONV_C - 1)
    cc = cw_ref[0:1, :] * vbuf[off:off + tt, :]
    for j in range(1, CONV_C):
        cc = cc + cw_ref[j:j + 1, :] * vbuf[off + j:off + j + tt, :]

    nc_ref[0] = vbuf[tt + off:tt + HALO_C, :]

    vbuf[0:HALO_C, :] = vbuf[tt:tt + HALO_C, :]
    y = _dot(h, w_ref[:, :D_MODEL]) * cc * _silu(_dot(h, w_ref[:, 3 * D_MODEL:]))
    yc_ref[0] = _dot(y.astype(BF16), wo_ref[...])


def _mixer_c_prompt(h, w, cw, wo, tt):
    bsz, seq, d = h.shape
    nt = seq // tt
    kern = functools.partial(_mixer_c_kernel, tt=tt)
    return pl.pallas_call(
        kern,
        grid=(bsz, nt),
        in_specs=[
            pl.BlockSpec((1, tt, d), lambda b, t: (b, t, 0)),
            w.spec(), _full(cw.shape), wo.spec(),
        ],
        out_specs=[
            pl.BlockSpec((1, tt, d), lambda b, t: (b, t, 0)),
            pl.BlockSpec((1, CONV_C - 1, d), lambda b, t: (b, 0, 0)),
        ],
        out_shape=[
            jax.ShapeDtypeStruct((bsz, seq, d), F32),
            jax.ShapeDtypeStruct((bsz, CONV_C - 1, d), F32),
        ],
        scratch_shapes=[pltpu.VMEM((HALO_C + tt, d), F32)],
        compiler_params=_params(("arbitrary", "arbitrary")),
        name="mixer_c_prompt",
    )(h, w.array, cw, wo.array)


def _mixer_c_sample_kernel(h_ref, st_ref, w_ref, cw_ref, wo_ref, yc_ref, nc_ref, *, steps, n):
    h = h_ref[...]
    p = _dot(h, w_ref[...])
    v = p[:, D_MODEL:2 * D_MODEL] * p[:, 2 * D_MODEL:3 * D_MODEL]
    past = CONV_C - 1
    slabs = [st_ref[:, i, :] for i in range(past)] + [v[s * n:(s + 1) * n, :] for s in range(steps)]
    rows = []
    for s in range(steps):
        cc = cw_ref[0:1, :] * slabs[s]
        for j in range(1, CONV_C):
            cc = cc + cw_ref[j:j + 1, :] * slabs[s + j]
        rows.append(cc)
    cc = jnp.concatenate(rows, axis=0)
    for i in range(past):
        nc_ref[:, i, :] = slabs[steps + i]
    y = p[:, :D_MODEL] * cc * _silu(p[:, 3 * D_MODEL:])
    yc_ref[...] = _dot(y.astype(BF16), wo_ref[...])


def _mixer_c_sample(h2, state, w, cw, wo, steps):
    rows, d = h2.shape
    n = rows // steps
    kern = functools.partial(_mixer_c_sample_kernel, steps=steps, n=n)
    return pl.pallas_call(
        kern,
        grid=(1,),
        in_specs=[_full((rows, d)), _full(state.shape), w.spec(), _full(cw.shape), wo.spec()],
        out_specs=[_full((rows, d)), _full(state.shape)],
        out_shape=[jax.ShapeDtypeStruct((rows, d), F32), jax.ShapeDtypeStruct(state.shape, F32)],
        compiler_params=_params(("arbitrary",)),
        name="mixer_c_sample",
    )(h2, state, w.array, cw, wo.array)


def _merge_kernel(x_ref, h_ref, ya_ref, yb_ref, yc_ref, w_ref, gb_ref, wo_ref, ng_ref, *out_refs, final):
    h = h_ref[0]
    m = None
    for i, y_ref in enumerate((ya_ref, yb_ref, yc_ref)):
        cs = slice(i * D_MODEL, (i + 1) * D_MODEL)
        term = _sigmoid(_dot(h, w_ref[:, cs]) + gb_ref[:, cs]) * y_ref[0]
        m = term if m is None else m + term
    out = x_ref[0] + _dot(m.astype(BF16), wo_ref[...])
    normed = _rms_rows(out, ng_ref[...])
    if final:
        out_refs[0][0] = normed
    else:
        out_refs[0][0] = out
        out_refs[1][0] = normed.astype(BF16)


def _merge(x, h, ya, yb, yc, w, gate_b, wo, next_g, tt, final):
    bsz, seq, d = x.shape
    tile = pl.BlockSpec((1, tt, d), lambda b, t: (b, t, 0))
    kern = functools.partial(_merge_kernel, final=final)
    out_shape = [jax.ShapeDtypeStruct((bsz, seq, d), F32)]
    if not final:
        out_shape.append(jax.ShapeDtypeStruct((bsz, seq, d), BF16))
    outs = pl.pallas_call(
        kern,
        grid=(bsz, seq // tt),
        in_specs=[tile, tile, tile, tile, tile, w.spec(), _full(gate_b.shape), wo.spec(),
                  _full((1, d))],
        out_specs=[tile] * len(out_shape),
        out_shape=out_shape,
        compiler_params=_params(("arbitrary", "arbitrary")),
        name="merge",
    )(x, h, ya, yb, yc, w.array, gate_b, wo.array, next_g)
    return (outs[0], None) if final else (outs[0], outs[1])


def _rmsnorm_kernel(x_ref, g_ref, h_ref):
    h_ref[0] = _rms_rows(x_ref[0], g_ref[...]).astype(BF16)


def _rmsnorm_bf16(x, g, tt):
    bsz, seq, d = x.shape
    tile = pl.BlockSpec((1, tt, d), lambda b, t: (b, t, 0))
    return pl.pallas_call(
        _rmsnorm_kernel,
        grid=(bsz, seq // tt),
        in_specs=[tile, _full((1, d))],
        out_specs=tile,
        out_shape=jax.ShapeDtypeStruct((bsz, seq, d), BF16),
        compiler_params=_params(("arbitrary", "arbitrary")),
        name="rmsnorm_bf16",
    )(x, g)


def _lower_bound(lbraw_ref, layer):
    depth = lbraw_ref.shape[0]
    rows = [lbraw_ref[i:i + 1, :] for i in range(depth)]
    mx = functools.reduce(jnp.maximum, rows)
    ex = [jnp.exp(r - mx) for r in rows]
    tot = functools.reduce(lambda a, b: a + b, ex)
    csum = []
    run = None
    for e in ex:
        sm = e / tot
        run = sm if run is None else run + sm
        csum.append(run)
    return csum[layer] - csum[0]


def _forget_terms(f_pre, lb):
    e = jnp.exp(-jnp.abs(f_pre))
    r = 1.0 / (1.0 + e)
    er = e * r
    pos = f_pre > 0.0
    log_sig = jnp.minimum(f_pre, 0.0) + jnp.log(r)
    one_m_lb = 1.0 - lb
    f = lb + one_m_lb * jnp.where(pos, r, er)
    logf = jnp.where(lb > 0.0, jnp.log(f), log_sig)
    k = one_m_lb * jnp.where(pos, er, r)
    return logf, k


def _split3(x):
    hi = x.astype(BF16)
    r1 = x - hi.astype(F32)
    mid = r1.astype(BF16)
    lo = (r1 - mid.astype(F32)).astype(BF16)
    return hi, mid, lo


def _hgrn_chunk_consts(c):
    levels = c.bit_length() - 1
    t = np.arange(c)
    col = t[None, :]
    row = t[:, None]
    blocks = [col <= row, col > row]
    masks = []
    for j in range(levels):
        half = 1 << j
        p = ((t >> (j + 1)) << (j + 1)) + half
        tgt = ((t >> j) & 1) == 1
        seg_t = (col >= p[:, None]) & (col <= row)
        seg_s = (col > row) & (col <= p[:, None] - 1)
        blocks.append(np.where(tgt[:, None], seg_t, seg_s))
        same = (row >> (j + 1)) == (col >> (j + 1))
        masks.append(same & tgt[:, None] & (~tgt)[None, :])
    assert (np.sum(np.stack(masks), axis=0) == (col < row)).all()
    seg = np.concatenate(blocks, axis=0).astype(np.float32)
    seg3 = np.concatenate([seg, seg, seg], axis=1)
    pair_masks = np.tile(np.stack(masks).astype(np.float32), (1, 1, 2))
    return jnp.asarray(seg3, BF16), jnp.asarray(pair_masks), levels


def _mixer_b_kernel(h_ref, w_ref, lbraw_ref, hgn_ref, wo_ref, seg_ref, mask_ref, *rest,
                    tt, layer, levels):
    yb_ref, nh_ref, q_s, k_s, lf_s, v_s, gz_s, y_s, dec2_s, att2_s, st_s = rest[-11:]
    t = pl.program_id(1)
    c = HG_CHUNK
    w_bk = HG_HEADS * HG_DK

    @pl.when(t == 0)
    def _():
        st_s[...] = jnp.zeros(st_s.shape, F32)

    h = h_ref[0]
    logf, kk = _forget_terms(_dot(h, w_ref[:, w_bk:2 * w_bk]), _lower_bound(lbraw_ref, layer))
    lf_s[...] = logf
    k_s[...] = kk
    gz_s[...] = _silu(_dot(h, w_ref[:, 3 * w_bk:]))
    q_s[...] = _dot(h, w_ref[:, :w_bk])
    v_s[...] = _dot(h, w_ref[:, 2 * w_bk:3 * w_bk]).astype(BF16)

    def chunk(ci, dec_s, att_s):
        rows = pl.ds(pl.multiple_of(ci * c, c), c)
        hi, mid, lo = _split3(lf_s[rows, :])
        expo = _dot(seg_ref[...], jnp.concatenate([hi, mid, lo], axis=0))
        dec_s[...] = jnp.exp(expo)
        zrow = jnp.zeros((c, HG_DK), BF16)
        zblk = jnp.zeros((HG_DV, HG_DK), BF16)

        def blockdiag(m, z):
            return jnp.concatenate([jnp.concatenate([m[:, :HG_DK], z], axis=1),
                                    jnp.concatenate([z, m[:, HG_DK:]], axis=1)], axis=0)

        for pr in range(HG_HEADS // 2):
            cs2 = slice(2 * pr * HG_DK, 2 * (pr + 1) * HG_DK)
            q2 = q_s[rows, cs2]
            k2 = k_s[rows, cs2]
            att = None
            for j in range(levels):
                gj = dec_s[(2 + j) * c:(3 + j) * c, cs2]
                kj = (k2 * gj).astype(BF16)
                pj = lax.dot_general((q2 * gj).astype(BF16), blockdiag(kj, zrow), NT_DIMS,
                                     preferred_element_type=F32) * mask_ref[j]
                att = pj if att is None else att + pj
            att_s[pr] = att.astype(BF16)

        for pr in range(HG_HEADS // 2):
            cs2 = slice(2 * pr * HG_DK, 2 * (pr + 1) * HG_DK)
            q2 = q_s[rows, cs2]
            k2 = k_s[rows, cs2]
            v2 = v_s[rows, cs2]
            qg = (q2 * dec_s[0:c, cs2]).astype(BF16)
            st_t = jnp.concatenate(
                [jnp.concatenate([st_s[2 * pr].astype(BF16), zblk], axis=1),
                 jnp.concatenate([zblk, st_s[2 * pr + 1].astype(BF16)], axis=1)], axis=0)
            o2 = lax.dot_general(qg, st_t, NT_DIMS, preferred_element_type=F32)
            o2 = o2 + _dot(att_s[pr], blockdiag(v2, zrow))
            qk = q2 * k2
            for i in range(2):
                hd = 2 * pr + i
                cs = slice(hd * HG_DK, (hd + 1) * HG_DK)
                hs = slice(i * HG_DK, (i + 1) * HG_DK)
                o = o2[:, hs] + jnp.sum(qk[:, hs], axis=-1, keepdims=True) * v2[:, hs].astype(F32)
                o = o * lax.rsqrt(jnp.mean(o * o, axis=-1, keepdims=True) + EPS) * hgn_ref[:, cs]
                y_s[rows, cs] = (o * gz_s[rows, cs]).astype(BF16)

        for pr in range(HG_HEADS // 2):
            cs2 = slice(2 * pr * HG_DK, 2 * (pr + 1) * HG_DK)
            khat = (k_s[rows, cs2] * dec_s[c:2 * c, cs2]).astype(BF16)
            upd = lax.dot_general(v_s[rows, cs2], khat, TN_DIMS, preferred_element_type=F32)
            last = dec_s[c - 1:c, cs2]
            for i in range(2):
                hs = slice(i * HG_DK, (i + 1) * HG_DK)
                st_s[2 * pr + i] = st_s[2 * pr + i] * last[:, hs] + upd[hs, hs]

    def chunk_group(cg, carry):
        for e in range(HG_SLOTS):
            chunk(HG_SLOTS * cg + e, dec2_s.at[e], att2_s.at[e])
        return carry

    lax.fori_loop(0, tt // (HG_SLOTS * c), chunk_group, 0)
    for hd in range(HG_HEADS):
        nh_ref[0, hd] = st_s[hd].T
    yb_ref[0] = _dot(y_s[...], wo_ref[...])


def _mixer_b_prompt(h, w, lb_raw, hgn, wo, prev, layer, depth, tt):
    bsz, seq, d = h.shape
    nt = seq // tt
    seg3, masks, levels = _hgrn_chunk_consts(HG_CHUNK)
    w_bk = HG_HEADS * HG_DK
    kern = functools.partial(_mixer_b_kernel, tt=tt, layer=layer, levels=levels)
    operands = [h, w.array, lb_raw, hgn, wo.array, seg3, masks]
    slot_specs, slot_args, aliases = _stacked_slot(prev, len(operands), 1)
    return pl.pallas_call(
        kern,
        grid=(bsz, nt),
        in_specs=[
            pl.BlockSpec((1, tt, d), lambda b, t: (b, t, 0)),
            w.spec(), _full(lb_raw.shape), _full((1, w_bk)), wo.spec(),
            _full(seg3.shape), _full(masks.shape),
        ] + slot_specs,
        out_specs=[
            pl.BlockSpec((1, tt, d), lambda b, t: (b, t, 0)),
            pl.BlockSpec((None, 1, HG_HEADS, HG_DK, HG_DV), lambda b, t: (layer, b, 0, 0, 0)),
        ],
        out_shape=[
            jax.ShapeDtypeStruct((bsz, seq, d), F32),
            jax.ShapeDtypeStruct((depth, bsz, HG_HEADS, HG_DK, HG_DV), F32),
        ],
        input_output_aliases=aliases,
        scratch_shapes=[
            pltpu.VMEM((tt, w_bk), F32),
            pltpu.VMEM((tt, w_bk), F32),
            pltpu.VMEM((tt, w_bk), F32),
            pltpu.VMEM((tt, w_bk), BF16),
            pltpu.VMEM((tt, w_bk), F32),
            pltpu.VMEM((tt, w_bk), BF16),
            pltpu.VMEM((HG_SLOTS, (levels + 2) * HG_CHUNK, w_bk), F32),
            pltpu.VMEM((HG_SLOTS, HG_HEADS // 2, HG_CHUNK, 2 * HG_CHUNK), BF16),
            pltpu.VMEM((HG_HEADS, HG_DV, HG_DK), F32),
        ],
        compiler_params=_params(("arbitrary", "arbitrary")),
        name="mixer_b_prompt",
    )(*operands, *slot_args)


def _mixer_b_sample_proj_kernel(h_ref, w_ref, lbraw_ref,
                                qg_ref, kh_ref, v_ref, eb_ref, oi_ref, gz_ref, *, steps, n, layer):
    w_bk = HG_HEADS * HG_DK
    h = h_ref[...]
    p = _dot(h, w_ref[...])
    logf, kk = _forget_terms(p[:, w_bk:2 * w_bk], _lower_bound(lbraw_ref, layer))
    q = p[:, :w_bk]
    v = p[:, 2 * w_bk:3 * w_bk]
    gz_ref[...] = _silu(p[:, 3 * w_bk:])
    v_ref[...] = v.astype(BF16)

    rs = lambda a, s: a[s * n:(s + 1) * n, :]
    bc = []
    for s in range(steps):
        bc.append(rs(logf, s) if s == 0 else bc[-1] + rs(logf, s))
    eb_ref[...] = jnp.exp(bc[-1])
    ones = jnp.ones((HG_DK, HG_DK), BF16)
    for s in range(steps):
        qg_ref[s * n:(s + 1) * n, :] = (rs(q, s) * jnp.exp(bc[s])).astype(BF16)
        kh_ref[s * n:(s + 1) * n, :] = (rs(kk, s) * jnp.exp(bc[-1] - bc[s])).astype(BF16)
        acc = jnp.zeros((n, w_bk), F32)
        for r in range(s + 1):
            pr = rs(q, s) * rs(kk, r)
            if r < s:
                pr = pr * jnp.exp(bc[s] - bc[r])
            pr = pr.astype(BF16)
            att = jnp.concatenate(
                [_dot(pr[:, hd * HG_DK:(hd + 1) * HG_DK], ones) for hd in range(HG_HEADS)], axis=1)
            acc = acc + att * rs(v, r)
        oi_ref[s * n:(s + 1) * n, :] = acc


def _mixer_b_sample_proj(h2, w, lb_raw, layer, steps):
    rows, d = h2.shape
    n = rows // steps
    w_bk = HG_HEADS * HG_DK
    kern = functools.partial(_mixer_b_sample_proj_kernel, steps=steps, n=n, layer=layer)
    return pl.pallas_call(
        kern,
        grid=(1,),
        in_specs=[_full((rows, d)), w.spec(), _full(lb_raw.shape)],
        out_specs=[_full((rows, w_bk)), _full((rows, w_bk)), _full((rows, w_bk)), _full((n, w_bk)),
                   _full((rows, w_bk)), _full((rows, w_bk))],
        out_shape=[
            jax.ShapeDtypeStruct((rows, w_bk), BF16),
            jax.ShapeDtypeStruct((rows, w_bk), BF16),
            jax.ShapeDtypeStruct((rows, w_bk), BF16),
            jax.ShapeDtypeStruct((n, w_bk), F32),
            jax.ShapeDtypeStruct((rows, w_bk), F32),
            jax.ShapeDtypeStruct((rows, w_bk), F32),
        ],
        compiler_params=_params(("arbitrary",)),
        name="mixer_b_sample_proj",
    )(h2, w.array, lb_raw)


def _mixer_b_sample_state_kernel(st_ref, qg_ref, kh_ref, v_ref, eb_ref, *rest, steps, nb):
    io_ref, ns_ref = rest[-2:]
    pad = SUBLANES - 3 - steps
    ones3 = jnp.ones((3, HG_DV), BF16)
    for i in range(nb):
        for hd in range(HG_HEADS):
            cs = slice(hd * HG_DK, (hd + 1) * HG_DK)
            s = st_ref[i, hd]
            io_ref[:, i, cs] = _dot(qg_ref[:, i, cs], s.astype(BF16))
            hi, mid, lo = _split3(eb_ref[i:i + 1, cs])
            lhs = jnp.concatenate([hi, mid, lo, kh_ref[:, i, cs], jnp.zeros((pad, HG_DK), BF16)], axis=0)
            top = jnp.concatenate([ones3, jnp.zeros((3, HG_DV), BF16)], axis=1)
            mid_rows = jnp.concatenate([jnp.zeros((steps, HG_DV), BF16), v_ref[:, i, cs]], axis=1)
            rhs = jnp.concatenate([top, mid_rows, jnp.zeros((pad, 2 * HG_DV), BF16)], axis=0)
            r = lax.dot_general(lhs, rhs, TN_DIMS, preferred_element_type=F32)
            ns_ref[i, hd] = r[:, :HG_DV] * s + r[:, HG_DV:]


def _mixer_b_sample_state(states, qg, kh, v, eb, prev, layer, nb):
    steps, n, w_bk = qg.shape
    kern = functools.partial(_mixer_b_sample_state_kernel, steps=steps, nb=nb)
    tok = pl.BlockSpec((steps, nb, w_bk), lambda i: (0, i, 0))
    st = pl.BlockSpec((None, nb, HG_HEADS, HG_DK, HG_DV), lambda i: (layer, i, 0, 0, 0))
    operands = [states, qg, kh, v, eb]
    slot_specs, slot_args, aliases = _stacked_slot(prev, len(operands), 1)
    return pl.pallas_call(
        kern,
        grid=(n // nb,),
        in_specs=[st, tok, tok, tok, pl.BlockSpec((nb, w_bk), lambda i: (i, 0))] + slot_specs,
        out_specs=[tok, st],
        out_shape=[jax.ShapeDtypeStruct((steps, n, w_bk), F32), jax.ShapeDtypeStruct(states.shape, F32)],
        input_output_aliases=aliases,
        compiler_params=_params(("arbitrary",)),
        name="mixer_b_sample_state",
    )(*operands, *slot_args)


def _mixer_b_sample_out_kernel(io_ref, oi_ref, gz_ref, hgn_ref, wo_ref, yb_ref):
    o = io_ref[...] + oi_ref[...]
    parts = []
    for hd in range(HG_HEADS):
        oh = o[:, hd * HG_DV:(hd + 1) * HG_DV]
        parts.append(oh * lax.rsqrt(jnp.mean(oh * oh, axis=-1, keepdims=True) + EPS))
    y = jnp.concatenate(parts, axis=1) * hgn_ref[...] * gz_ref[...]
    yb_ref[...] = _dot(y.astype(BF16), wo_ref[...])


def _mixer_b_sample_out(inter, intra, gz, hgn, wo):
    rows, w_bv = inter.shape
    return pl.pallas_call(
        _mixer_b_sample_out_kernel,
        grid=(1,),
        in_specs=[_full((rows, w_bv))] * 3 + [_full((1, w_bv)), wo.spec()],
        out_specs=_full((rows, wo.width)),
        out_shape=jax.ShapeDtypeStruct((rows, wo.width), F32),
        compiler_params=_params(("arbitrary",)),
        name="mixer_b_sample_out",
    )(inter, intra, gz, hgn, wo.array)


def kernel(x_prompt, x_sample, state_conv_a, state_hgrn, state_conv_c, norm_g, w_in, gate_b, conv_a_w,
           conv_a_b, ln_g, ln_b, w_a_out, lb_raw, hg_norm_g, w_b_out, conv_c_w, w_c_out, w_o, final_norm_g):
    depth = w_in.shape[0]
    d = x_prompt.shape[-1]
    n_seq, steps, _ = x_sample.shape
    w_bk = HG_HEADS * HG_DK
    w_bv = HG_HEADS * HG_DV
    col_a = 3 * d
    col_b = col_a + 2 * w_bk + 2 * w_bv
    col_c = col_b + 4 * d
    row = lambda a: a.reshape(1, -1)
    fin_g = row(final_norm_g)

    rows = steps * n_seq
    xp = x_prompt
    xs = jnp.swapaxes(x_sample, 0, 1).reshape(1, rows, d)
    hp = _rmsnorm_bf16(xp, row(norm_g[0]), tt=xp.shape[1])
    hs = _rmsnorm_bf16(xs, row(norm_g[0]), tt=rows)
    conv_a_tm = jnp.swapaxes(state_conv_a, 1, 2)
    pa, pc, sc = [], [], []
    ph = sa = sh = None
    stack_bf16 = lambda w: w.astype(BF16).reshape(depth * w.shape[1], w.shape[2])
    w_in_bf = stack_bf16(w_in)
    out_bf = [stack_bf16(w) for w in (w_a_out, w_b_out, w_c_out, w_o)]
    for l in range(depth):
        cols = lambda a, b: _Cols(w_in_bf, l * d, d, a, b - a)
        w_a, w_b, w_c, w_g = (cols(0, col_a), cols(col_a, col_b), cols(col_b, col_c),
                              cols(col_c, w_in.shape[-1]))
        wao, wbo, wco, woo = (_Cols(w, l * w.shape[0] // depth, w.shape[0] // depth, 0, w.shape[1])
                              for w in out_bf)
        gb = row(gate_b[l])
        cab, lng, lnb, hgn = row(conv_a_b[l]), row(ln_g[l]), row(ln_b[l]), row(hg_norm_g[l])
        final = l == depth - 1
        next_g = fin_g if final else row(norm_g[l + 1])

        ya, na = _mixer_a_prompt(hp, w_a, conv_a_w[l], cab, lng, lnb, wao, tt=PROMPT_TILE)
        yb, ph = _mixer_b_prompt(hp, w_b, lb_raw, hgn, wbo, ph, layer=l, depth=depth, tt=PROMPT_TILE)
        yc, nc = _mixer_c_prompt(hp, w_c, conv_c_w[l], wco, tt=PROMPT_TILE_C)
        xp, hp = _merge(xp, hp, ya, yb, yc, w_g, gb, woo, next_g, tt=PROMPT_TILE, final=final)
        pa.append(na); pc.append(nc)

        hs2 = hs.reshape(rows, d)
        ya, sa = _mixer_a_sample(hs.reshape(steps, n_seq, d), conv_a_tm, w_a, conv_a_w[l], cab, lng, lnb,
                                 wao, sa, layer=l, nb=SAMPLE_CONV_SEQS)
        qg, kh, v, eb, intra, gz = _mixer_b_sample_proj(hs2, w_b, lb_raw, layer=l, steps=steps)
        tm = lambda a: a.reshape(steps, n_seq, w_bk)
        inter, sh = _mixer_b_sample_state(state_hgrn, tm(qg), tm(kh), tm(v), eb, sh, layer=l,
                                          nb=SAMPLE_STATE_SEQS)
        yb = _mixer_b_sample_out(inter.reshape(rows, w_bv), intra, gz, hgn, wbo)
        yc, nc = _mixer_c_sample(hs2, state_conv_c[l], w_c, conv_c_w[l], wco, steps=steps)
        tile = lambda a: a.reshape(1, rows, d)
        xs, hs = _merge(xs, hs, tile(ya), tile(yb), tile(yc), w_g, gb, woo, next_g, tt=rows, final=final)
        sc.append(nc)

    y_sample = jnp.swapaxes(xs.reshape(steps, n_seq, d), 0, 1)
    return (xp, y_sample, jnp.stack(pa), ph, jnp.stack(pc), jnp.swapaxes(sa, 1, 2), sh, jnp.stack(sc))
```
